```python
import jax, jax.numpy as jnp
from jax import lax
import numpy as np

D_MODEL = 1024
BATCH = 1
SEQ = 16384
DEPTH = 4

GRID_W = 64
CTX_LEN = 256
N_MIXERS = 2
N_A = (DEPTH + N_MIXERS - 1) // N_MIXERS
N_B = DEPTH // N_MIXERS
LRU_WIDTH = D_MODEL
LRU_HEADS = 4
LRU_BLOCK = LRU_WIDTH // LRU_HEADS
LRU_C = 8.0
CONV_A_WIDTH = 4
CONV_A_PAD = 2
CONV_B_WIDTH = 3
FFN_CONV_WIDTH = 3
D_FF = 2816
N_MOD = 6
EPS = 1e-6

kernel_name = "hybrid_rglru_shortconv_dit_prefix"


def rms_norm(x, g):
    xf = x.astype(jnp.float32)
    y = xf * lax.rsqrt(jnp.mean(xf * xf, axis=-1, keepdims=True) + EPS) * g.astype(jnp.float32)
    return y.astype(x.dtype)


def modulate(h, shift, scale):
    return h * (1.0 + scale) + shift


def adaln(cvec, w, b):
    return jnp.split(jax.nn.silu(cvec) @ w + b, N_MOD, axis=-1)


def dwconv(x, w, b, pad_lo, axis):
    k_width = w.shape[0]
    n = x.shape[axis]
    pads = [(0, 0)] * x.ndim
    pads[axis] = (pad_lo, k_width - 1 - pad_lo)
    xp = jnp.pad(x, pads)
    out = b + w[0] * lax.slice_in_dim(xp, 0, n, axis=axis)
    for k in range(1, k_width):
        out = out + w[k] * lax.slice_in_dim(xp, k, k + n, axis=axis)
    return out


def seq_conv(x, w, b, pad_lo):
    return dwconv(x, w, b, pad_lo, axis=1)


def grid_conv(x, w, b, pad_lo, rows, grid_axis):
    bsz, n, ch = x.shape
    g = x.reshape(bsz, rows, GRID_W, ch)
    return dwconv(g, w, b, pad_lo, axis=grid_axis).reshape(bsz, n, ch)


def _lin_combine(e1, e2):
    a1, b1 = e1
    a2, b2 = e2
    return a1 * a2, a2 * b1 + b2


def linear_scan(a, b, h0, reverse):
    a_cum, b_cum = lax.associative_scan(_lin_combine, (a, b), reverse=reverse, axis=1)
    if h0 is None:
        return b_cum
    return b_cum + a_cum * h0[:, None, :]


def rglru_coeffs(u, w_r, b_r, w_i, b_i, lam):
    bsz, n, width = u.shape
    ub = u.reshape(bsz, n, LRU_HEADS, LRU_BLOCK)
    r = jax.nn.sigmoid(jnp.einsum('bnhi,hij->bnhj', ub, w_r) + b_r).reshape(bsz, n, width).astype(jnp.float32)
    i = jax.nn.sigmoid(jnp.einsum('bnhi,hij->bnhj', ub, w_i) + b_i).reshape(bsz, n, width).astype(jnp.float32)
    log_a = LRU_C * r * jax.nn.log_sigmoid(lam.astype(jnp.float32))
    a = jnp.exp(log_a)
    b = jnp.sqrt(-jnp.expm1(2.0 * log_a)) * (i * u.astype(jnp.float32))
    return a, b


def rglru_direction(ul, uc, w_r, b_r, w_i, b_i, lam, reverse):
    a_c, b_c = rglru_coeffs(uc, w_r, b_r, w_i, b_i, lam)
    h_c = linear_scan(a_c, b_c, None, reverse)
    h0 = h_c[:, 0] if reverse else h_c[:, -1]
    a_l, b_l = rglru_coeffs(ul, w_r, b_r, w_i, b_i, lam)
    h_l = linear_scan(a_l, b_l, h0, reverse)
    return h_l, h_c


def rglru_mixer(hl, hc, w_x, w_gate, conv_w, conv_b, w_r, b_r, w_i, b_i, lam, w_out, ctx_out):
    ul = seq_conv(hl @ w_x, conv_w, conv_b, CONV_A_PAD)
    uc = seq_conv(hc @ w_x, conv_w, conv_b, CONV_A_PAD)
    hl_f, hc_f = rglru_direction(ul, uc, w_r[0], b_r[0], w_i[0], b_i[0], lam[0], False)
    hl_b, hc_b = rglru_direction(ul, uc, w_r[1], b_r[1], w_i[1], b_i[1], lam[1], True)
    yl = (jax.nn.gelu(hl @ w_gate) * (hl_f + hl_b).astype(hl.dtype)) @ w_out
    yc = None
    if ctx_out:
        yc = (jax.nn.gelu(hc @ w_gate) * (hc_f + hc_b).astype(hc.dtype)) @ w_out
    return yl, yc


def shortconv_mixer(h, w_in, conv_w, conv_b, w_out, conv_fn):
    g_b, g_c, v = jnp.split(h @ w_in, 3, axis=-1)
    return (g_b * conv_fn(g_c * v, conv_w, conv_b)) @ w_out


def conv_ffn(h, w_up, conv_w, conv_b, w_down, conv_fn):
    g, v = jnp.split(h @ w_up, 2, axis=-1)
    return (jax.nn.silu(conv_fn(g, conv_w, conv_b)) * v) @ w_down


def setup_inputs(seed: int = 0) -> dict:
    key = jax.random.key(seed)
    ks = jax.random.split(key, 32)
    f32 = jnp.float32
    D = D_MODEL

    def nrm(k, shape, scale):
        return jax.random.normal(k, shape, f32) * scale

    lam_a = jax.random.uniform(ks[14], (N_A, 2, LRU_WIDTH), f32, 0.9, 0.999)
    s = lam_a ** (1.0 / LRU_C)
    a_lambda = jnp.log(s) - jnp.log1p(-s)
    return {
        "x": nrm(ks[0], (BATCH, SEQ, D), 1.0),
        "c": nrm(ks[1], (BATCH, D), 1.0),
        "ctx": nrm(ks[2], (BATCH, CTX_LEN, D), 1.0),
        "c_ctx": nrm(ks[3], (D,), 1.0),
        "ada_w": nrm(ks[4], (DEPTH, D, N_MOD * D), D ** -0.5),
        "ada_b": nrm(ks[5], (DEPTH, N_MOD * D), 0.02),
        "norm_mix_g": 1.0 + nrm(ks[6], (DEPTH, D), 0.02),
        "norm_ffn_g": 1.0 + nrm(ks[7], (DEPTH, D), 0.02),
        "a_w_x": nrm(ks[8], (N_A, D, LRU_WIDTH), D ** -0.5),
        "a_w_gate": nrm(ks[9], (N_A, D, LRU_WIDTH), D ** -0.5),
        "a_conv_w": nrm(ks[10], (N_A, CONV_A_WIDTH, LRU_WIDTH), CONV_A_WIDTH ** -0.5),
        "a_conv_b": nrm(ks[11], (N_A, LRU_WIDTH), 0.01),
        "a_w_r": nrm(ks[12], (N_A, 2, LRU_HEADS, LRU_BLOCK, LRU_BLOCK), LRU_BLOCK ** -0.5),
        "a_b_r": nrm(ks[13], (N_A, 2, LRU_HEADS, LRU_BLOCK), 0.01),
        "a_w_i": nrm(ks[15], (N_A, 2, LRU_HEADS, LRU_BLOCK, LRU_BLOCK), LRU_BLOCK ** -0.5),
        "a_b_i": nrm(ks[16], (N_A, 2, LRU_HEADS, LRU_BLOCK), 0.01),
        "a_lambda": a_lambda,
        "a_w_out": nrm(ks[17], (N_A, LRU_WIDTH, D), LRU_WIDTH ** -0.5),
        "b_w_in": nrm(ks[18], (N_B, D, 3 * D), D ** -0.5),
        "b_conv_w": nrm(ks[19], (N_B, CONV_B_WIDTH, D), CONV_B_WIDTH ** -0.5),
        "b_conv_b": nrm(ks[20], (N_B, D), 0.01),
        "b_w_out": nrm(ks[21], (N_B, D, D), D ** -0.5),
        "f_w_up": nrm(ks[22], (DEPTH, D, 2 * D_FF), D ** -0.5),
        "f_conv_w": nrm(ks[23], (DEPTH, FFN_CONV_WIDTH, D_FF), FFN_CONV_WIDTH ** -0.5),
        "f_conv_b": nrm(ks[24], (DEPTH, D_FF), 0.01),
        "f_w_down": nrm(ks[25], (DEPTH, D_FF, D), D_FF ** -0.5),
        "final_g": 1.0 + nrm(ks[26], (D,), 0.02),
    }


def reference(x, c, ctx, c_ctx, ada_w, ada_b, norm_mix_g, norm_ffn_g,
              a_w_x, a_w_gate, a_conv_w, a_conv_b, a_w_r, a_b_r, a_w_i, a_b_i, a_lambda, a_w_out,
              b_w_in, b_conv_w, b_conv_b, b_w_out,
              f_w_up, f_conv_w, f_conv_b, f_w_down, final_g):
    rows = x.shape[1] // GRID_W

    def latent_vconv(t, w, b):
        return grid_conv(t, w, b, 1, rows, 1)

    def latent_hconv(t, w, b):
        return grid_conv(t, w, b, 1, rows, 2)

    def ctx_conv(t, w, b):
        return seq_conv(t, w, b, 1)

    h_ctx = ctx
    for l in range(DEPTH):
        j = l // N_MIXERS
        ctx_needed = any((k % N_MIXERS) == 0 for k in range(l + 1, DEPTH))
        sm, cm, gm, sf, cf, gf = [m[:, None, :] for m in adaln(c, ada_w[l], ada_b[l])]
        smc, cmc, gmc, sfc, cfc, gfc = adaln(c_ctx, ada_w[l], ada_b[l])

        hl = modulate(rms_norm(x, norm_mix_g[l]), sm, cm)
        yc = None
        if l % N_MIXERS == 0:
            hc = modulate(rms_norm(h_ctx, norm_mix_g[l]), smc, cmc)
            yl, yc = rglru_mixer(hl, hc, a_w_x[j], a_w_gate[j], a_conv_w[j], a_conv_b[j],
                                 a_w_r[j], a_b_r[j], a_w_i[j], a_b_i[j], a_lambda[j], a_w_out[j],
                                 ctx_needed)
        else:
            yl = shortconv_mixer(hl, b_w_in[j], b_conv_w[j], b_conv_b[j], b_w_out[j], latent_vconv)
            if ctx_needed:
                hc = modulate(rms_norm(h_ctx, norm_mix_g[l]), smc, cmc)
                yc = shortconv_mixer(hc, b_w_in[j], b_conv_w[j], b_conv_b[j], b_w_out[j], ctx_conv)
        x = x + (gm * yl).astype(x.dtype)
        if ctx_needed:
            h_ctx = h_ctx + (gmc * yc).astype(h_ctx.dtype)

        hl = modulate(rms_norm(x, norm_ffn_g[l]), sf, cf)
        x = x + (gf * conv_ffn(hl, f_w_up[l], f_conv_w[l], f_conv_b[l], f_w_down[l], latent_hconv)).astype(x.dtype)
        if ctx_needed:
            hc = modulate(rms_norm(h_ctx, norm_ffn_g[l]), sfc, cfc)
            h_ctx = h_ctx + (gfc * conv_ffn(hc, f_w_up[l], f_conv_w[l], f_conv_b[l], f_w_down[l], ctx_conv)).astype(h_ctx.dtype)

    return rms_norm(x, final_g)
```

```python
import functools

import jax
import jax.numpy as jnp
from jax import lax
from jax.experimental import pallas as pl
from jax.experimental.pallas import tpu as pltpu

F32 = jnp.float32
BF16 = jnp.bfloat16

EPS = 1e-6
LRU_C = 8.0
LRU_HEADS = 4
GRID_W = 64
N_MOD = 6
MOD_ROWS = 8
SUBLANES = 8
SEQ_HALO = 8
TOKEN_TILE = 512
FF_CHUNK = 256
ADA_COLS = 1536
VMEM_LIMIT = 56 * 1024 * 1024


def _dot(a, b):
    return jnp.dot(a, b, preferred_element_type=F32)


def _norm_mod(x, g, shift, scale):
    ms = jnp.mean(x * x, axis=-1, keepdims=True)
    return (x * lax.rsqrt(ms + EPS) * g) * (1.0 + scale) + shift


def _shift_rows(x, s):
    n = x.shape[0]
    return pltpu.roll(x, s % n, 0)


def _const_spec(shape):
    zeros = (0,) * len(shape)
    return pl.BlockSpec(shape, lambda *_: zeros, pipeline_mode=pl.Buffered(1))


def _params(n_axes):
    return pltpu.CompilerParams(
        dimension_semantics=("arbitrary",) * n_axes, vmem_limit_bytes=VMEM_LIMIT)


def _ada_kernel(c_ref, w_ref, b_ref, o_ref):
    cv = c_ref[...]
    s = jax.nn.silu(cv).astype(BF16)
    o_ref[0] = _dot(s, w_ref[0].astype(BF16)) + b_ref[0]


def _adaln_all(cvecs, ada_w, ada_b):
    depth, d, n = ada_w.shape
    return pl.pallas_call(
        _ada_kernel,
        grid=(depth, n // ADA_COLS),
        in_specs=[
            pl.BlockSpec((MOD_ROWS, d), lambda l, c: (0, 0)),
            pl.BlockSpec((1, d, ADA_COLS), lambda l, c: (l, 0, c)),
            pl.BlockSpec((1, 1, ADA_COLS), lambda l, c: (l, 0, c)),
        ],
        out_specs=pl.BlockSpec((1, MOD_ROWS, ADA_COLS), lambda l, c: (l, 0, c)),
        out_shape=jax.ShapeDtypeStruct((depth, MOD_ROWS, n), F32),
        compiler_params=_params(2),
        name="adaln",
    )(cvecs, ada_w, ada_b.reshape(depth, 1, n))


def _ffn_kernel(x_ref, mod_ref, g_ref, wup_ref, cw_ref, cb_ref, wdn_ref, fg_ref, o_ref,
                act_ref, *, period, final_norm):
    x = x_ref[...]
    tt = x.shape[0]
    d_ff = wdn_ref.shape[0]
    h = _norm_mod(x, g_ref[...], mod_ref[3:4, :], mod_ref[4:5, :]).astype(BF16)
    pos = lax.broadcasted_iota(jnp.int32, (tt, FF_CHUNK), 0) % period
    has_prev = pos != 0
    has_next = pos != period - 1
    for c in range(d_ff // FF_CHUNK):
        lo = c * FF_CHUNK
        g = _dot(h, wup_ref[:, lo:lo + FF_CHUNK])
        v = _dot(h, wup_ref[:, d_ff + lo:d_ff + lo + FF_CHUNK])
        g_prev = jnp.where(has_prev, _shift_rows(g, 1), 0.0)
        g_next = jnp.where(has_next, _shift_rows(g, -1), 0.0)
        cw = cw_ref[:, lo:lo + FF_CHUNK]
        cv = (cb_ref[:, lo:lo + FF_CHUNK] + cw[0:1, :] * g_prev
              + cw[1:2, :] * g + cw[2:3, :] * g_next)
        act_ref[:, lo:lo + FF_CHUNK] = (jax.nn.silu(cv) * v).astype(BF16)
    y = x + mod_ref[5:6, :] * _dot(act_ref[...], wdn_ref[...])
    if final_norm:
        ms = jnp.mean(y * y, axis=-1, keepdims=True)
        y = y * lax.rsqrt(ms + EPS) * fg_ref[...]
    o_ref[...] = y


def _conv_ffn(x, mod, gain, w_up, conv_w, conv_b, w_down, final_g, *, tile, period,
              final_norm):
    t, d = x.shape
    d_ff = w_down.shape[0]
    row = lambda k: (k, 0)
    return pl.pallas_call(
        functools.partial(_ffn_kernel, period=period, final_norm=final_norm),
        grid=(t // tile,),
        in_specs=[
            pl.BlockSpec((tile, d), row),
            _const_spec((MOD_ROWS, d)),
            _const_spec((1, d)),
            _const_spec((d, 2 * d_ff)),
            _const_spec((3, d_ff)),
            _const_spec((1, d_ff)),
            _const_spec((d_ff, d)),
            _const_spec((1, d)),
        ],
        out_specs=pl.BlockSpec((tile, d), row),
        out_shape=jax.ShapeDtypeStruct((t, d), F32),
        scratch_shapes=[pltpu.VMEM((tile, d_ff), BF16)],
        compiler_params=_params(1),
        name="conv_ffn",
    )(x, mod, gain, w_up, conv_w, conv_b, w_down, final_g)


def _halo_maps(tile, halo, t, tile_of_step):
    per_tile = tile // halo
    last = t // halo - 1
    prev = lambda k: (jnp.maximum(tile_of_step(k) * per_tile - 1, 0), 0)
    nxt = lambda k: (jnp.minimum((tile_of_step(k) + 1) * per_tile, last), 0)
    return prev, nxt


def _with_halo(xp_ref, x_ref, xn_ref):
    return jnp.concatenate([xp_ref[...], x_ref[...], xn_ref[...]], axis=0)


def _zero_outside(v, halo, tile_idx, n_tiles):
    n = v.shape[0]
    head = jnp.where(tile_idx > 0, v[:halo], 0.0)
    tail = jnp.where(tile_idx < n_tiles - 1, v[n - halo:], 0.0)
    return jnp.concatenate([head, v[halo:n - halo], tail], axis=0)


def _shortconv_kernel(xp_ref, x_ref, xn_ref, mod_ref, g_ref, win_ref, cw_ref, cb_ref,
                      wout_ref, o_ref, *, shift, halo):
    k = pl.program_id(0)
    x = x_ref[...]
    tt, d = x.shape
    xe = _with_halo(xp_ref, x_ref, xn_ref)
    he = _norm_mod(xe, g_ref[...], mod_ref[0:1, :], mod_ref[1:2, :]).astype(BF16)
    cv = _dot(he, win_ref[:, d:3 * d])
    z = _zero_outside(cv[:, :d] * cv[:, d:], halo, k, pl.num_programs(0))
    gate_b = _dot(he[halo:halo + tt], win_ref[:, 0:d])
    conv = (cb_ref[...]
            + cw_ref[0:1, :] * _shift_rows(z, shift)[halo:halo + tt]
            + cw_ref[1:2, :] * z[halo:halo + tt]
            + cw_ref[2:3, :] * _shift_rows(z, -shift)[halo:halo + tt])
    y = _dot((gate_b * conv).astype(BF16), wout_ref[...])
    o_ref[...] = x + mod_ref[2:3, :] * y


def _shortconv(x, mod, gain, w_in, conv_w, conv_b, w_out, *, tile, shift):
    t, d = x.shape
    halo = max(shift, SUBLANES)
    prev, nxt = _halo_maps(tile, halo, t, lambda k: k)
    row = lambda k: (k, 0)
    return pl.pallas_call(
        functools.partial(_shortconv_kernel, shift=shift, halo=halo),
        grid=(t // tile,),
        in_specs=[
            pl.BlockSpec((halo, d), prev),
            pl.BlockSpec((tile, d), row),
            pl.BlockSpec((halo, d), nxt),
            _const_spec((MOD_ROWS, d)),
            _const_spec((1, d)),
            _const_spec((d, 3 * d)),
            _const_spec((3, d)),
            _const_spec((1, d)),
            _const_spec((d, d)),
        ],
        out_specs=pl.BlockSpec((tile, d), row),
        out_shape=jax.ShapeDtypeStruct((t, d), F32),
        compiler_params=_params(1),
        name="shortconv",
    )(x, x, x, mod, gain, w_in, conv_w, conv_b, w_out)


def _rglru_coeffs(u, wr_ref, br_ref, wi_ref, bi_ref, lam_ref):
    hb = wr_ref.shape[1]
    ub = u.astype(BF16)
    r_parts, i_parts = [], []
    for hd in range(LRU_HEADS):
        uh = ub[:, hd * hb:(hd + 1) * hb]
        r_parts.append(_dot(uh, wr_ref[hd]))
        i_parts.append(_dot(uh, wi_ref[hd]))
    r = jax.nn.sigmoid(jnp.concatenate(r_parts, axis=-1) + br_ref[...])
    i = jax.nn.sigmoid(jnp.concatenate(i_parts, axis=-1) + bi_ref[...])
    lam = lam_ref[...]
    log_sig = jnp.minimum(lam, 0.0) - jnp.log1p(jnp.exp(-jnp.abs(lam)))
    log_a = LRU_C * r * log_sig
    a = jnp.exp(log_a)
    one_minus_a2 = -(jnp.tanh(log_a) * (a * a + 1.0))
    b = jnp.sqrt(one_minus_a2) * (i * u)
    return a, b


def _scan_tile(a, b, a_ref, b_ref, h_ref, carry_ref, reverse):
    tt = a.shape[0]
    sub = lax.broadcasted_iota(jnp.int32, a.shape, 0) % SUBLANES
    for step in (1, 2, 4):
        if reverse:
            keep = sub < SUBLANES - step
            a_nb, b_nb = _shift_rows(a, -step), _shift_rows(b, -step)
        else:
            keep = sub >= step
            a_nb, b_nb = _shift_rows(a, step), _shift_rows(b, step)
        b = jnp.where(keep, a * b_nb + b, b)
        a = jnp.where(keep, a * a_nb, a)
    a_ref[...] = a
    b_ref[...] = b
    groups = tt // SUBLANES
    edge = 0 if reverse else SUBLANES - 1

    def body(gi, carry):
        grp = groups - 1 - gi if reverse else gi
        r0 = pl.multiple_of(grp * SUBLANES, SUBLANES)
        h = a_ref[pl.ds(r0, SUBLANES), :] * carry + b_ref[pl.ds(r0, SUBLANES), :]
        h_ref[pl.ds(r0, SUBLANES), :] = h
        return h[edge:edge + 1, :]

    carry = lax.fori_loop(0, groups, body, carry_ref[0:1, :], unroll=4)
    carry_ref[...] = jnp.broadcast_to(carry, carry_ref.shape)


def _rglru_bwd_kernel(xp_ref, x_ref, xn_ref, mod_ref, g_ref, wx_ref, cw_ref, cb_ref,
                      wr_ref, br_ref, wi_ref, bi_ref, lam_ref, h0_ref,
                      u_ref, hb_ref, cout_ref, a_ref, b_ref, carry_ref):
    k = pl.program_id(0)
    n_tiles = pl.num_programs(0)
    tt = x_ref.shape[0]

    @pl.when(k == 0)
    def _():
        carry_ref[...] = h0_ref[...]

    xe = _with_halo(xp_ref, x_ref, xn_ref)
    he = _norm_mod(xe, g_ref[...], mod_ref[0:1, :], mod_ref[1:2, :]).astype(BF16)
    pre = _zero_outside(_dot(he, wx_ref[...]), SEQ_HALO, n_tiles - 1 - k, n_tiles)
    lo, hi = SEQ_HALO, SEQ_HALO + tt
    u = (cb_ref[...]
         + cw_ref[0:1, :] * _shift_rows(pre, 2)[lo:hi]
         + cw_ref[1:2, :] * _shift_rows(pre, 1)[lo:hi]
         + cw_ref[2:3, :] * pre[lo:hi]
         + cw_ref[3:4, :] * _shift_rows(pre, -1)[lo:hi])
    u_ref[...] = u
    a, b = _rglru_coeffs(u, wr_ref, br_ref, wi_ref, bi_ref, lam_ref)
    _scan_tile(a, b, a_ref, b_ref, hb_ref, carry_ref, reverse=True)
    cout_ref[...] = carry_ref[...]


def _rglru_fwd_kernel(x_ref, u_ref, hb_ref, mod_ref, g_ref, wg_ref,
                      wr_ref, br_ref, wi_ref, bi_ref, lam_ref, wout_ref, h0_ref,
                      o_ref, cout_ref, a_ref, b_ref, hf_ref, carry_ref):
    @pl.when(pl.program_id(0) == 0)
    def _():
        carry_ref[...] = h0_ref[...]

    x = x_ref[...]
    h = _norm_mod(x, g_ref[...], mod_ref[0:1, :], mod_ref[1:2, :]).astype(BF16)
    gate = jax.nn.gelu(_dot(h, wg_ref[...]))
    a, b = _rglru_coeffs(u_ref[...], wr_ref, br_ref, wi_ref, bi_ref, lam_ref)
    _scan_tile(a, b, a_ref, b_ref, hf_ref, carry_ref, reverse=False)
    cout_ref[...] = carry_ref[...]
    mixed = (gate * (hf_ref[...] + hb_ref[...])).astype(BF16)
    o_ref[...] = x + mod_ref[2:3, :] * _dot(mixed, wout_ref[...])


def _gate_specs(w):
    hb = w // LRU_HEADS
    return [
        _const_spec((LRU_HEADS, hb, hb)),
        _const_spec((1, w)),
        _const_spec((LRU_HEADS, hb, hb)),
        _const_spec((1, w)),
        _const_spec((1, w)),
    ]


def _rglru_bwd(x, mod, gain, w_x, conv_w, conv_b, gates, h0, *, tile):
    t, d = x.shape
    w = w_x.shape[1]
    n_tiles = t // tile
    rev = lambda k: n_tiles - 1 - k
    prev, nxt = _halo_maps(tile, SEQ_HALO, t, rev)
    row = lambda k: (rev(k), 0)
    return pl.pallas_call(
        _rglru_bwd_kernel,
        grid=(n_tiles,),
        in_specs=[
            pl.BlockSpec((SEQ_HALO, d), prev),
            pl.BlockSpec((tile, d), row),
            pl.BlockSpec((SEQ_HALO, d), nxt),
            _const_spec((MOD_ROWS, d)),
            _const_spec((1, d)),
            _const_spec((d, w)),
            _const_spec((4, w)),
            _const_spec((1, w)),
            *_gate_specs(w),
            _const_spec((SUBLANES, w)),
        ],
        out_specs=[
            pl.BlockSpec((tile, w), row),
            pl.BlockSpec((tile, w), row),
            pl.BlockSpec((SUBLANES, w), lambda k: (0, 0)),
        ],
        out_shape=[
            jax.ShapeDtypeStruct((t, w), F32),
            jax.ShapeDtypeStruct((t, w), F32),
            jax.ShapeDtypeStruct((SUBLANES, w), F32),
        ],
        scratch_shapes=[
            pltpu.VMEM((tile, w), F32),
            pltpu.VMEM((tile, w), F32),
            pltpu.VMEM((SUBLANES, w), F32),
        ],
        compiler_params=_params(1),
        name="rglru_bwd",
    )(x, x, x, mod, gain, w_x, conv_w, conv_b, *gates, h0)


def _rglru_fwd(x, u, hb, mod, gain, w_gate, gates, w_out, h0, *, tile):
    t, d = x.shape
    w = w_gate.shape[1]
    row = lambda k: (k, 0)
    return pl.pallas_call(
        _rglru_fwd_kernel,
        grid=(t // tile,),
        in_specs=[
            pl.BlockSpec((tile, d), row),
            pl.BlockSpec((tile, w), row),
            pl.BlockSpec((tile, w), row),
            _const_spec((MOD_ROWS, d)),
            _const_spec((1, d)),
            _const_spec((d, w)),
            *_gate_specs(w),
            _const_spec((w, d)),
            _const_spec((SUBLANES, w)),
        ],
        out_specs=[
            pl.BlockSpec((tile, d), row),
            pl.BlockSpec((SUBLANES, w), lambda k: (0, 0)),
        ],
        out_shape=[
            jax.ShapeDtypeStruct((t, d), F32),
            jax.ShapeDtypeStruct((SUBLANES, w), F32),
        ],
        scratch_shapes=[
            pltpu.VMEM((tile, w), F32),
            pltpu.VMEM((tile, w), F32),
            pltpu.VMEM((tile, w), F32),
            pltpu.VMEM((SUBLANES, w), F32),
        ],
        compiler_params=_params(1),
        name="rglru_fwd",
    )(x, u, hb, mod, gain, w_gate, *gates, w_out, h0)


def kernel(x, c, ctx, c_ctx, ada_w, ada_b, norm_mix_g, norm_ffn_g, a_w_x, a_w_gate, a_conv_w, a_conv_b, a_w_r, a_b_r, a_w_i, a_b_i, a_lambda, a_w_out, b_w_in, b_conv_w, b_conv_b, b_w_out, f_w_up, f_conv_w, f_conv_b, f_w_down, final_g):
    bsz, seq, d = x.shape
    ctx_len = ctx.shape[1]
    depth = ada_w.shape[0]
    n_mixers = 2
    assert bsz == 1 and seq % TOKEN_TILE == 0 and TOKEN_TILE % GRID_W == 0
    lat_tile, ctx_tile = TOKEN_TILE, ctx_len

    cvecs = jnp.zeros((MOD_ROWS, d), F32).at[0].set(c[0]).at[1].set(c_ctx)
    mods = _adaln_all(cvecs, ada_w, ada_b).reshape(depth, MOD_ROWS, N_MOD, d)
    mods = jnp.pad(mods[:, :2], ((0, 0), (0, 0), (0, MOD_ROWS - N_MOD), (0, 0)))

    row = lambda v: v.reshape(1, -1)
    xl, xc = x[0], ctx[0]
    zero_state = jnp.zeros((SUBLANES, a_w_x.shape[2]), F32)
    for l in range(depth):
        j = l // n_mixers
        ctx_needed = any(k % n_mixers == 0 for k in range(l + 1, depth))
        mod_l, mod_c = mods[l, 0], mods[l, 1]
        g_mix = row(norm_mix_g[l])
        if l % n_mixers == 0:
            w_x, w_gate, w_out = (a_w_x[j].astype(BF16), a_w_gate[j].astype(BF16),
                                  a_w_out[j].astype(BF16))
            conv_w, conv_b = a_conv_w[j], row(a_conv_b[j])
            gates = [(a_w_r[j, dr].astype(BF16), row(a_b_r[j, dr]), a_w_i[j, dr].astype(BF16),
                      row(a_b_i[j, dr]), row(a_lambda[j, dr])) for dr in range(2)]
            u_c, hb_c, end_b = _rglru_bwd(xc, mod_c, g_mix, w_x, conv_w, conv_b, gates[1],
                                          zero_state, tile=ctx_tile)
            xc_new, end_f = _rglru_fwd(xc, u_c, hb_c, mod_c, g_mix, w_gate, gates[0], w_out,
                                       zero_state, tile=ctx_tile)
            u_l, hb_l, _ = _rglru_bwd(xl, mod_l, g_mix, w_x, conv_w, conv_b, gates[1],
                                      end_b, tile=lat_tile)
            xl, _ = _rglru_fwd(xl, u_l, hb_l, mod_l, g_mix, w_gate, gates[0], w_out,
                               end_f, tile=lat_tile)
            if ctx_needed:
                xc = xc_new
        else:
            w_in, w_out = b_w_in[j].astype(BF16), b_w_out[j].astype(BF16)
            conv_w, conv_b = b_conv_w[j], row(b_conv_b[j])
            xl = _shortconv(xl, mod_l, g_mix, w_in, conv_w, conv_b, w_out,
                            tile=lat_tile, shift=GRID_W)
            if ctx_needed:
                xc = _shortconv(xc, mod_c, g_mix, w_in, conv_w, conv_b, w_out,
                                tile=ctx_tile, shift=1)
        g_ffn = row(norm_ffn_g[l])
        w_up, w_down = f_w_up[l].astype(BF16), f_w_down[l].astype(BF16)
        conv_w, conv_b = f_conv_w[l], row(f_conv_b[l])
        xl = _conv_ffn(xl, mod_l, g_ffn, w_up, conv_w, conv_b, w_down, row(final_g),
                       tile=lat_tile, period=GRID_W, final_norm=(l == depth - 1))
        if ctx_needed:
            xc = _conv_ffn(xc, mod_c, g_ffn, w_up, conv_w, conv_b, w_down, row(final_g),
                           tile=ctx_tile, period=ctx_len, final_norm=False)
    return xl[None]
```

```python
import functools

import jax
import jax.numpy as jnp
from jax import lax
from jax.experimental import pallas as pl
from jax.experimental.pallas import tpu as pltpu

F32 = jnp.float32
BF16 = jnp.bfloat16

EPS = 1e-6
LRU_C = 8.0
LRU_HEADS = 4
GRID_W = 64
N_MOD = 6
MOD_ROWS = 8
SUBLANES = 8
LANES = 128
SEQ_HALO = 8
TOKEN_TILE = 512
FF_CHUNK = 256
ADA_COLS = 1536
VMEM_LIMIT = 56 * 1024 * 1024


def _dot(a, b):
    return jnp.dot(a, b, preferred_element_type=F32)


def _norm_mod(x, g, shift, scale):
    ms = jnp.mean(x * x, axis=-1, keepdims=True)
    return (x * lax.rsqrt(ms + EPS) * g) * (1.0 + scale) + shift


def _shift_rows(x, s):
    n = x.shape[0]
    return pltpu.roll(x, s % n, 0)


def _const_spec(shape):
    zeros = (0,) * len(shape)
    return pl.BlockSpec(shape, lambda *_: zeros, pipeline_mode=pl.Buffered(1))


def _params(n_axes):
    return pltpu.CompilerParams(
        dimension_semantics=("arbitrary",) * n_axes, vmem_limit_bytes=VMEM_LIMIT)


def _ada_kernel(c_ref, w_ref, b_ref, o_ref):
    cv = c_ref[...]
    s = jax.nn.silu(cv).astype(BF16)
    o_ref[0] = _dot(s, w_ref[0].astype(BF16)) + b_ref[0]


def _adaln_all(cvecs, ada_w, ada_b):
    depth, d, n = ada_w.shape
    return pl.pallas_call(
        _ada_kernel,
        grid=(depth, n // ADA_COLS),
        in_specs=[
            pl.BlockSpec((MOD_ROWS, d), lambda l, c: (0, 0)),
            pl.BlockSpec((1, d, ADA_COLS), lambda l, c: (l, 0, c)),
            pl.BlockSpec((1, 1, ADA_COLS), lambda l, c: (l, 0, c)),
        ],
        out_specs=pl.BlockSpec((1, MOD_ROWS, ADA_COLS), lambda l, c: (l, 0, c)),
        out_shape=jax.ShapeDtypeStruct((depth, MOD_ROWS, n), F32),
        compiler_params=_params(2),
        name="adaln",
    )(cvecs, ada_w, ada_b.reshape(depth, 1, n))


def _ffn_kernel(x_ref, mod_ref, g_ref, wup_ref, cw_ref, cb_ref, wdn_ref, fg_ref, o_ref,
                act_ref, *, period, final_norm):
    x = x_ref[...]
    tt = x.shape[0]
    d_ff = wdn_ref.shape[0]
    h = _norm_mod(x, g_ref[...], mod_ref[3:4, :], mod_ref[4:5, :]).astype(BF16)
    pos = lax.broadcasted_iota(jnp.int32, (tt, FF_CHUNK), 0) % period
    has_prev = pos != 0
    has_next = pos != period - 1
    for c in range(d_ff // FF_CHUNK):
        lo = c * FF_CHUNK
        g = _dot(h, wup_ref[:, lo:lo + FF_CHUNK])
        v = _dot(h, wup_ref[:, d_ff + lo:d_ff + lo + FF_CHUNK])
        g_prev = jnp.where(has_prev, _shift_rows(g, 1), 0.0)
        g_next = jnp.where(has_next, _shift_rows(g, -1), 0.0)
        cw = cw_ref[:, lo:lo + FF_CHUNK]
        cv = (cb_ref[:, lo:lo + FF_CHUNK] + cw[0:1, :] * g_prev
              + cw[1:2, :] * g + cw[2:3, :] * g_next)
        act_ref[:, lo:lo + FF_CHUNK] = (jax.nn.silu(cv) * v).astype(BF16)
    y = x + mod_ref[5:6, :] * _dot(act_ref[...], wdn_ref[...])
    if final_norm:
        ms = jnp.mean(y * y, axis=-1, keepdims=True)
        y = y * lax.rsqrt(ms + EPS) * fg_ref[...]
    o_ref[...] = y


def _conv_ffn(x, mod, gain, w_up, conv_w, conv_b, w_down, final_g, *, tile, period,
              final_norm):
    t, d = x.shape
    d_ff = w_down.shape[0]
    row = lambda k: (k, 0)
    return pl.pallas_call(
        functools.partial(_ffn_kernel, period=period, final_norm=final_norm),
        grid=(t // tile,),
        in_specs=[
            pl.BlockSpec((tile, d), row),
            _const_spec((MOD_ROWS, d)),
            _const_spec((1, d)),
            _const_spec((d, 2 * d_ff)),
            _const_spec((3, d_ff)),
            _const_spec((1, d_ff)),
            _const_spec((d_ff, d)),
            _const_spec((1, d)),
        ],
        out_specs=pl.BlockSpec((tile, d), row),
        out_shape=jax.ShapeDtypeStruct((t, d), F32),
        scratch_shapes=[pltpu.VMEM((tile, d_ff), BF16)],
        compiler_params=_params(1),
        name="conv_ffn",
    )(x, mod, gain, w_up, conv_w, conv_b, w_down, final_g)


def _halo_maps(tile, halo, t, tile_of_step):
    per_tile = tile // halo
    last = t // halo - 1
    prev = lambda k: (jnp.maximum(tile_of_step(k) * per_tile - 1, 0), 0)
    nxt = lambda k: (jnp.minimum((tile_of_step(k) + 1) * per_tile, last), 0)
    return prev, nxt


def _with_halo(xp_ref, x_ref, xn_ref):
    return jnp.concatenate([xp_ref[...], x_ref[...], xn_ref[...]], axis=0)


def _zero_outside(v, halo, tile_idx, n_tiles):
    n = v.shape[0]
    head = jnp.where(tile_idx > 0, v[:halo], 0.0)
    tail = jnp.where(tile_idx < n_tiles - 1, v[n - halo:], 0.0)
    return jnp.concatenate([head, v[halo:n - halo], tail], axis=0)


def _shortconv_kernel(xp_ref, x_ref, xn_ref, mod_ref, g_ref, win_ref, cw_ref, cb_ref,
                      wout_ref, o_ref, *, shift, halo):
    k = pl.program_id(0)
    x = x_ref[...]
    tt, d = x.shape
    xe = _with_halo(xp_ref, x_ref, xn_ref)
    he = _norm_mod(xe, g_ref[...], mod_ref[0:1, :], mod_ref[1:2, :]).astype(BF16)
    cv = _dot(he, win_ref[:, d:3 * d])
    z = _zero_outside(cv[:, :d] * cv[:, d:], halo, k, pl.num_programs(0))
    gate_b = _dot(he[halo:halo + tt], win_ref[:, 0:d])
    conv = (cb_ref[...]
            + cw_ref[0:1, :] * _shift_rows(z, shift)[halo:halo + tt]
            + cw_ref[1:2, :] * z[halo:halo + tt]
            + cw_ref[2:3, :] * _shift_rows(z, -shift)[halo:halo + tt])
    y = _dot((gate_b * conv).astype(BF16), wout_ref[...])
    o_ref[...] = x + mod_ref[2:3, :] * y


def _shortconv(x, mod, gain, w_in, conv_w, conv_b, w_out, *, tile, shift):
    t, d = x.shape
    halo = max(shift, SUBLANES)
    prev, nxt = _halo_maps(tile, halo, t, lambda k: k)
    row = lambda k: (k, 0)
    return pl.pallas_call(
        functools.partial(_shortconv_kernel, shift=shift, halo=halo),
        grid=(t // tile,),
        in_specs=[
            pl.BlockSpec((halo, d), prev),
            pl.BlockSpec((tile, d), row),
            pl.BlockSpec((halo, d), nxt),
            _const_spec((MOD_ROWS, d)),
            _const_spec((1, d)),
            _const_spec((d, 3 * d)),
            _const_spec((3, d)),
            _const_spec((1, d)),
            _const_spec((d, d)),
        ],
        out_specs=pl.BlockSpec((tile, d), row),
        out_shape=jax.ShapeDtypeStruct((t, d), F32),
        compiler_params=_params(1),
        name="shortconv",
    )(x, x, x, mod, gain, w_in, conv_w, conv_b, w_out)


def _rglru_coeffs(u, wr_ref, br_ref, wi_ref, bi_ref, lam_ref):
    hb = wr_ref.shape[1]
    ub = u.astype(BF16)
    r_parts, i_parts = [], []
    for hd in range(LRU_HEADS):
        uh = ub[:, hd * hb:(hd + 1) * hb]
        r_parts.append(_dot(uh, wr_ref[hd]))
        i_parts.append(_dot(uh, wi_ref[hd]))
    r = jax.nn.sigmoid(jnp.concatenate(r_parts, axis=-1) + br_ref[...])
    i = jax.nn.sigmoid(jnp.concatenate(i_parts, axis=-1) + bi_ref[...])
    lam = lam_ref[...]
    log_sig = jnp.minimum(lam, 0.0) - jnp.log1p(jnp.exp(-jnp.abs(lam)))
    log_a = LRU_C * r * log_sig
    a = jnp.exp(log_a)
    one_minus_a2 = -(jnp.tanh(log_a) * (a * a + 1.0))
    b = jnp.sqrt(one_minus_a2) * (i * u)
    return a, b


def _slab_scratch(tile, w):
    return pltpu.VMEM((w // LANES, tile, LANES), F32)


def _lockstep_blocks(tile):
    seg = tile // SUBLANES
    return [(s, q) for s in range(SUBLANES) for q in range(seg // SUBLANES)], seg


def _to_lockstep(v, slab_ref):
    tt, w = v.shape
    blocks, seg = _lockstep_blocks(tt)
    for j in range(w // LANES):
        for s, q in blocks:
            r0 = s * seg + q * SUBLANES
            slab_ref[j, pl.ds(q * SUBLANES * SUBLANES + s, SUBLANES, stride=SUBLANES), :] = (
                v[r0:r0 + SUBLANES, j * LANES:(j + 1) * LANES])
    return jnp.concatenate([slab_ref[j] for j in range(w // LANES)], axis=1)


def _from_lockstep(slab_ref):
    n_lane, tt, _ = slab_ref.shape
    blocks, _ = _lockstep_blocks(tt)
    return jnp.concatenate(
        [jnp.concatenate(
            [slab_ref[j, pl.ds(q * SUBLANES * SUBLANES + s, SUBLANES, stride=SUBLANES), :]
             for s, q in blocks], axis=0) for j in range(n_lane)], axis=1)


def _scan_lockstep(a_ref, b_ref, carry_ref, reverse, emit):
    tt, w = a_ref.shape
    seg = tt // SUBLANES

    def rows(ref, i):
        return ref[pl.ds(pl.multiple_of(i * SUBLANES, SUBLANES), SUBLANES), :]

    def step_of(i):
        return seg - 1 - i if reverse else i

    def totals(i, carry):
        prod, acc = carry
        av = rows(a_ref, step_of(i))
        return av * prod, av * acc + rows(b_ref, step_of(i))

    init = (jnp.ones((SUBLANES, w), F32), jnp.zeros((SUBLANES, w), F32))
    prod, acc = lax.fori_loop(0, seg, totals, init, unroll=2)

    state = carry_ref[0:1, :]
    starts = [None] * SUBLANES
    for s in (range(SUBLANES - 1, -1, -1) if reverse else range(SUBLANES)):
        starts[s] = state
        state = prod[s:s + 1, :] * state + acc[s:s + 1, :]
    carry_ref[...] = jnp.broadcast_to(state, carry_ref.shape)

    def apply(i, h):
        t = step_of(i)
        h = rows(a_ref, t) * h + rows(b_ref, t)
        emit(t, h)
        return h

    lax.fori_loop(0, seg, apply, jnp.concatenate(starts, axis=0), unroll=2)


def _rglru_bwd_kernel(xp_ref, x_ref, xn_ref, mod_ref, g_ref, wx_ref, cw_ref, cb_ref,
                      wr_ref, br_ref, wi_ref, bi_ref, lam_ref, h0_ref,
                      u_ref, hb_ref, cout_ref, a_ref, b_ref, slab_ref, carry_ref):
    k = pl.program_id(0)
    n_tiles = pl.num_programs(0)
    tt = x_ref.shape[0]

    @pl.when(k == 0)
    def _():
        carry_ref[...] = h0_ref[...]

    xe = _with_halo(xp_ref, x_ref, xn_ref)
    he = _norm_mod(xe, g_ref[...], mod_ref[0:1, :], mod_ref[1:2, :]).astype(BF16)
    pre = _zero_outside(_dot(he, wx_ref[...]), SEQ_HALO, n_tiles - 1 - k, n_tiles)
    lo, hi = SEQ_HALO, SEQ_HALO + tt
    u = (cb_ref[...]
         + cw_ref[0:1, :] * _shift_rows(pre, 2)[lo:hi]
         + cw_ref[1:2, :] * _shift_rows(pre, 1)[lo:hi]
         + cw_ref[2:3, :] * pre[lo:hi]
         + cw_ref[3:4, :] * _shift_rows(pre, -1)[lo:hi])
    u = _to_lockstep(u, slab_ref)
    u_ref[...] = u
    a, b = _rglru_coeffs(u, wr_ref, br_ref, wi_ref, bi_ref, lam_ref)
    a_ref[...] = a
    b_ref[...] = b

    def emit(i, h):
        hb_ref[pl.ds(pl.multiple_of(i * SUBLANES, SUBLANES), SUBLANES), :] = h

    _scan_lockstep(a_ref, b_ref, carry_ref, True, emit)
    cout_ref[...] = carry_ref[...]


def _rglru_fwd_kernel(x_ref, u_ref, hb_ref, mod_ref, g_ref, wg_ref,
                      wr_ref, br_ref, wi_ref, bi_ref, lam_ref, wout_ref, h0_ref,
                      o_ref, cout_ref, a_ref, b_ref, slab_ref, carry_ref):
    @pl.when(pl.program_id(0) == 0)
    def _():
        carry_ref[...] = h0_ref[...]

    x = x_ref[...]
    h = _norm_mod(x, g_ref[...], mod_ref[0:1, :], mod_ref[1:2, :]).astype(BF16)
    gate = jax.nn.gelu(_dot(h, wg_ref[...]))
    a, b = _rglru_coeffs(u_ref[...], wr_ref, br_ref, wi_ref, bi_ref, lam_ref)
    a_ref[...] = a
    b_ref[...] = b

    def emit(i, hf):
        r0 = pl.multiple_of(i * SUBLANES, SUBLANES)
        both = hf + hb_ref[pl.ds(r0, SUBLANES), :]
        for j in range(slab_ref.shape[0]):
            slab_ref[j, pl.ds(r0, SUBLANES), :] = both[:, j * LANES:(j + 1) * LANES]

    _scan_lockstep(a_ref, b_ref, carry_ref, False, emit)
    cout_ref[...] = carry_ref[...]
    mixed = (gate * _from_lockstep(slab_ref)).astype(BF16)
    o_ref[...] = x + mod_ref[2:3, :] * _dot(mixed, wout_ref[...])


def _gate_specs(w):
    hb = w // LRU_HEADS
    return [
        _const_spec((LRU_HEADS, hb, hb)),
        _const_spec((1, w)),
        _const_spec((LRU_HEADS, hb, hb)),
        _const_spec((1, w)),
        _const_spec((1, w)),
    ]


def _rglru_bwd(x, mod, gain, w_x, conv_w, conv_b, gates, h0, *, tile):
    t, d = x.shape
    w = w_x.shape[1]
    n_tiles = t // tile
    rev = lambda k: n_tiles - 1 - k
    prev, nxt = _halo_maps(tile, SEQ_HALO, t, rev)
    row = lambda k: (rev(k), 0)
    return pl.pallas_call(
        _rglru_bwd_kernel,
        grid=(n_tiles,),
        in_specs=[
            pl.BlockSpec((SEQ_HALO, d), prev),
            pl.BlockSpec((tile, d), row),
            pl.BlockSpec((SEQ_HALO, d), nxt),
            _const_spec((MOD_ROWS, d)),
            _const_spec((1, d)),
            _const_spec((d, w)),
            _const_spec((4, w)),
            _const_spec((1, w)),
            *_gate_specs(w),
            _const_spec((SUBLANES, w)),
        ],
        out_specs=[
            pl.BlockSpec((tile, w), row),
            pl.BlockSpec((tile, w), row),
            pl.BlockSpec((SUBLANES, w), lambda k: (0, 0)),
        ],
        out_shape=[
            jax.ShapeDtypeStruct((t, w), F32),
            jax.ShapeDtypeStruct((t, w), F32),
            jax.ShapeDtypeStruct((SUBLANES, w), F32),
        ],
        scratch_shapes=[
            pltpu.VMEM((tile, w), F32),
            pltpu.VMEM((tile, w), F32),
            _slab_scratch(tile, w),
            pltpu.VMEM((SUBLANES, w), F32),
        ],
        compiler_params=_params(1),
        name="rglru_bwd",
    )(x, x, x, mod, gain, w_x, conv_w, conv_b, *gates, h0)


def _rglru_fwd(x, u, hb, mod, gain, w_gate, gates, w_out, h0, *, tile):
    t, d = x.shape
    w = w_gate.shape[1]
    row = lambda k: (k, 0)
    return pl.pallas_call(
        _rglru_fwd_kernel,
        grid=(t // tile,),
        in_specs=[
            pl.BlockSpec((tile, d), row),
            pl.BlockSpec((tile, w), row),
            pl.BlockSpec((tile, w), row),
            _const_spec((MOD_ROWS, d)),
            _const_spec((1, d)),
            _const_spec((d, w)),
            *_gate_specs(w),
            _const_spec((w, d)),
            _const_spec((SUBLANES, w)),
        ],
        out_specs=[
            pl.BlockSpec((tile, d), row),
            pl.BlockSpec((SUBLANES, w), lambda k: (0, 0)),
        ],
        out_shape=[
            jax.ShapeDtypeStruct((t, d), F32),
            jax.ShapeDtypeStruct((SUBLANES, w), F32),
        ],
        scratch_shapes=[
            pltpu.VMEM((tile, w), F32),
            pltpu.VMEM((tile, w), F32),
            _slab_scratch(tile, w),
            pltpu.VMEM((SUBLANES, w), F32),
        ],
        compiler_params=_params(1),
        name="rglru_fwd",
    )(x, u, hb, mod, gain, w_gate, *gates, w_out, h0)


def kernel(x, c, ctx, c_ctx, ada_w, ada_b, norm_mix_g, norm_ffn_g, a_w_x, a_w_gate, a_conv_w, a_conv_b, a_w_r, a_b_r, a_w_i, a_b_i, a_lambda, a_w_out, b_w_in, b_conv_w, b_conv_b, b_w_out, f_w_up, f_conv_w, f_conv_b, f_w_down, final_g):
    bsz, seq, d = x.shape
    ctx_len = ctx.shape[1]
    depth = ada_w.shape[0]
    n_mixers = 2
    assert bsz == 1 and seq % TOKEN_TILE == 0 and TOKEN_TILE % GRID_W == 0
    lat_tile, ctx_tile = TOKEN_TILE, ctx_len

    cvecs = jnp.zeros((MOD_ROWS, d), F32).at[0].set(c[0]).at[1].set(c_ctx)
    mods = _adaln_all(cvecs, ada_w, ada_b).reshape(depth, MOD_ROWS, N_MOD, d)
    mods = jnp.pad(mods[:, :2], ((0, 0), (0, 0), (0, MOD_ROWS - N_MOD), (0, 0)))

    row = lambda v: v.reshape(1, -1)
    xl, xc = x[0], ctx[0]
    zero_state = jnp.zeros((SUBLANES, a_w_x.shape[2]), F32)
    for l in range(depth):
        j = l // n_mixers
        ctx_needed = any(k % n_mixers == 0 for k in range(l + 1, depth))
        mod_l, mod_c = mods[l, 0], mods[l, 1]
        g_mix = row(norm_mix_g[l])
        if l % n_mixers == 0:
            w_x, w_gate, w_out = (a_w_x[j].astype(BF16), a_w_gate[j].astype(BF16),
                                  a_w_out[j].astype(BF16))
            conv_w, conv_b = a_conv_w[j], row(a_conv_b[j])
            gates = [(a_w_r[j, dr].astype(BF16), row(a_b_r[j, dr]), a_w_i[j, dr].astype(BF16),
                      row(a_b_i[j, dr]), row(a_lambda[j, dr])) for dr in range(2)]
            u_c, hb_c, end_b = _rglru_bwd(xc, mod_c, g_mix, w_x, conv_w, conv_b, gates[1],
                                          zero_state, tile=ctx_tile)
            xc_new, end_f = _rglru_fwd(xc, u_c, hb_c, mod_c, g_mix, w_gate, gates[0], w_out,
                                       zero_state, tile=ctx_tile)
            u_l, hb_l, _ = _rglru_bwd(xl, mod_l, g_mix, w_x, conv_w, conv_b, gates[1],
                                      end_b, tile=lat_tile)
            xl, _ = _rglru_fwd(xl, u_l, hb_l, mod_l, g_mix, w_gate, gates[0], w_out,
                               end_f, tile=lat_tile)
            if ctx_needed:
                xc = xc_new
        else:
            w_in, w_out = b_w_in[j].astype(BF16), b_w_out[j].astype(BF16)
            conv_w, conv_b = b_conv_w[j], row(b_conv_b[j])
            xl = _shortconv(xl, mod_l, g_mix, w_in, conv_w, conv_b, w_out,
                            tile=lat_tile, shift=GRID_W)
            if ctx_needed:
                xc = _shortconv(xc, mod_c, g_mix, w_in, conv_w, conv_b, w_out,
                                tile=ctx_tile, shift=1)
        g_ffn = row(norm_ffn_g[l])
        w_up, w_down = f_w_up[l].astype(BF16), f_w_down[l].astype(BF16)
        conv_w, conv_b = f_conv_w[l], row(f_conv_b[l])
        xl = _conv_ffn(xl, mod_l, g_ffn, w_up, conv_w, conv_b, w_down, row(final_g),
                       tile=lat_tile, period=GRID_W, final_norm=(l == depth - 1))
        if ctx_needed:
            xc = _conv_ffn(xc, mod_c, g_ffn, w_up, conv_w, conv_b, w_down, row(final_g),
                           tile=ctx_tile, period=ctx_len, final_norm=False)
    return xl[None]
```

```python
import functools

import jax
import jax.numpy as jnp
from jax import lax
from jax.experimental import pallas as pl
from jax.experimental.pallas import tpu as pltpu

F32 = jnp.float32
BF16 = jnp.bfloat16

EPS = 1e-6
LRU_C = 8.0
LRU_HEADS = 4
GRID_W = 64
N_MOD = 6
MOD_ROWS = 8
SUBLANES = 8
LANES = 128
SEQ_HALO = 8
TOKEN_TILE = 512
FFN_TILE = 512
LOG2_E = 1.4426950408889634
FF_CHUNK = 256
ADA_COLS = 1536
VMEM_LIMIT = 56 * 1024 * 1024


def _dot(a, b):
    return jnp.dot(a, b, preferred_element_type=F32)


def _norm_mod(x, g, shift, scale):
    ms = jnp.mean(x * x, axis=-1, keepdims=True)
    return (x * lax.rsqrt(ms + EPS) * g) * (1.0 + scale) + shift


def _shift_rows(x, s):
    n = x.shape[0]
    return pltpu.roll(x, s % n, 0)


def _const_spec(shape):
    zeros = (0,) * len(shape)
    return pl.BlockSpec(shape, lambda *_: zeros, pipeline_mode=pl.Buffered(1))


def _params(n_axes):
    return pltpu.CompilerParams(
        dimension_semantics=("arbitrary",) * n_axes, vmem_limit_bytes=VMEM_LIMIT)


def _ada_kernel(c_ref, w_ref, b_ref, o_ref):
    cv = c_ref[...]
    s = jax.nn.silu(cv).astype(BF16)
    o_ref[0] = _dot(s, w_ref[0].astype(BF16)) + b_ref[0]


def _adaln_all(cvecs, ada_w, ada_b):
    depth, d, n = ada_w.shape
    return pl.pallas_call(
        _ada_kernel,
        grid=(depth, n // ADA_COLS),
        in_specs=[
            pl.BlockSpec((MOD_ROWS, d), lambda l, c: (0, 0)),
            pl.BlockSpec((1, d, ADA_COLS), lambda l, c: (l, 0, c)),
            pl.BlockSpec((1, 1, ADA_COLS), lambda l, c: (l, 0, c)),
        ],
        out_specs=pl.BlockSpec((1, MOD_ROWS, ADA_COLS), lambda l, c: (l, 0, c)),
        out_shape=jax.ShapeDtypeStruct((depth, MOD_ROWS, n), F32),
        compiler_params=_params(2),
        name="adaln",
    )(cvecs, ada_w, ada_b.reshape(depth, 1, n))


def _ffn_kernel(x_ref, mod_ref, g_ref, wup_ref, cw_ref, cb_ref, wdn_ref, fg_ref, o_ref,
                act_ref, *, period, final_norm):
    x = x_ref[...]
    tt = x.shape[0]
    d_ff = wdn_ref.shape[0]
    h = _norm_mod(x, g_ref[...], mod_ref[3:4, :], mod_ref[4:5, :]).astype(BF16)
    pos = lax.broadcasted_iota(jnp.int32, (tt, FF_CHUNK), 0) % period
    has_prev = pos != 0
    has_next = pos != period - 1
    for c in range(d_ff // FF_CHUNK):
        lo = c * FF_CHUNK
        g = _dot(h, wup_ref[:, lo:lo + FF_CHUNK])
        v = _dot(h, wup_ref[:, d_ff + lo:d_ff + lo + FF_CHUNK])
        g_prev = jnp.where(has_prev, _shift_rows(g, 1), 0.0)
        g_next = jnp.where(has_next, _shift_rows(g, -1), 0.0)
        cw = cw_ref[:, lo:lo + FF_CHUNK]
        cv = (cb_ref[:, lo:lo + FF_CHUNK] + cw[0:1, :] * g_prev
              + cw[1:2, :] * g + cw[2:3, :] * g_next)
        act_ref[:, lo:lo + FF_CHUNK] = (_silu(cv) * v).astype(BF16)
    y = x + mod_ref[5:6, :] * _dot(act_ref[...], wdn_ref[...])
    if final_norm:
        ms = jnp.mean(y * y, axis=-1, keepdims=True)
        y = y * lax.rsqrt(ms + EPS) * fg_ref[...]
    o_ref[...] = y


def _conv_ffn(x, mod, gain, w_up, conv_w, conv_b, w_down, final_g, *, tile, period,
              final_norm):
    t, d = x.shape
    d_ff = w_down.shape[0]
    row = lambda k: (k, 0)
    return pl.pallas_call(
        functools.partial(_ffn_kernel, period=period, final_norm=final_norm),
        grid=(t // tile,),
        in_specs=[
            pl.BlockSpec((tile, d), row),
            _const_spec((MOD_ROWS, d)),
            _const_spec((1, d)),
            _const_spec((d, 2 * d_ff)),
            _const_spec((3, d_ff)),
            _const_spec((1, d_ff)),
            _const_spec((d_ff, d)),
            _const_spec((1, d)),
        ],
        out_specs=pl.BlockSpec((tile, d), row),
        out_shape=jax.ShapeDtypeStruct((t, d), F32),
        scratch_shapes=[pltpu.VMEM((tile, d_ff), BF16)],
        compiler_params=_params(1),
        name="conv_ffn",
    )(x, mod, gain, w_up, conv_w, conv_b, w_down, final_g)


def _halo_maps(tile, halo, t, tile_of_step):
    per_tile = tile // halo
    last = t // halo - 1
    prev = lambda k: (jnp.maximum(tile_of_step(k) * per_tile - 1, 0), 0)
    nxt = lambda k: (jnp.minimum((tile_of_step(k) + 1) * per_tile, last), 0)
    return prev, nxt


def _with_halo(xp_ref, x_ref, xn_ref):
    return jnp.concatenate([xp_ref[...], x_ref[...], xn_ref[...]], axis=0)


def _zero_outside(v, halo, tile_idx, n_tiles):
    n = v.shape[0]
    head = jnp.where(tile_idx > 0, v[:halo], 0.0)
    tail = jnp.where(tile_idx < n_tiles - 1, v[n - halo:], 0.0)
    return jnp.concatenate([head, v[halo:n - halo], tail], axis=0)


def _shortconv_kernel(xp_ref, x_ref, xn_ref, mod_ref, g_ref, win_ref, cw_ref, cb_ref,
                      wout_ref, o_ref, *, shift, halo):
    k = pl.program_id(0)
    x = x_ref[...]
    tt, d = x.shape
    xe = _with_halo(xp_ref, x_ref, xn_ref)
    he = _norm_mod(xe, g_ref[...], mod_ref[0:1, :], mod_ref[1:2, :]).astype(BF16)
    cv = _dot(he, win_ref[:, d:3 * d])
    z = _zero_outside(cv[:, :d] * cv[:, d:], halo, k, pl.num_programs(0))
    gate_b = _dot(he[halo:halo + tt], win_ref[:, 0:d])
    conv = (cb_ref[...]
            + cw_ref[0:1, :] * _shift_rows(z, shift)[halo:halo + tt]
            + cw_ref[1:2, :] * z[halo:halo + tt]
            + cw_ref[2:3, :] * _shift_rows(z, -shift)[halo:halo + tt])
    y = _dot((gate_b * conv).astype(BF16), wout_ref[...])
    o_ref[...] = x + mod_ref[2:3, :] * y


def _shortconv(x, mod, gain, w_in, conv_w, conv_b, w_out, *, tile, shift):
    t, d = x.shape
    halo = max(shift, SUBLANES)
    prev, nxt = _halo_maps(tile, halo, t, lambda k: k)
    row = lambda k: (k, 0)
    return pl.pallas_call(
        functools.partial(_shortconv_kernel, shift=shift, halo=halo),
        grid=(t // tile,),
        in_specs=[
            pl.BlockSpec((halo, d), prev),
            pl.BlockSpec((tile, d), row),
            pl.BlockSpec((halo, d), nxt),
            _const_spec((MOD_ROWS, d)),
            _const_spec((1, d)),
            _const_spec((d, 3 * d)),
            _const_spec((3, d)),
            _const_spec((1, d)),
            _const_spec((d, d)),
        ],
        out_specs=pl.BlockSpec((tile, d), row),
        out_shape=jax.ShapeDtypeStruct((t, d), F32),
        compiler_params=_params(1),
        name="shortconv",
    )(x, x, x, mod, gain, w_in, conv_w, conv_b, w_out)


def _rglru_coeffs(u, wr_ref, br_ref, wi_ref, bi_ref, lam_ref):
    hb = wr_ref.shape[1]
    ub = u.astype(BF16)
    r_parts, i_parts = [], []
    for hd in range(LRU_HEADS):
        uh = ub[:, hd * hb:(hd + 1) * hb]
        r_parts.append(_dot(uh, wr_ref[hd]))
        i_parts.append(_dot(uh, wi_ref[hd]))
    r2 = jnp.tanh(0.5 * (jnp.concatenate(r_parts, axis=-1) + br_ref[...])) + 1.0
    i2 = jnp.tanh(0.5 * (jnp.concatenate(i_parts, axis=-1) + bi_ref[...])) + 1.0
    lam = lam_ref[...]
    log_sig = jnp.minimum(lam, 0.0) - jnp.log1p(jnp.exp(-jnp.abs(lam)))
    half_c = (0.5 * LRU_C) * log_sig
    a = jnp.exp2(r2 * (half_c * LOG2_E))
    one_minus_a2 = jnp.tanh(r2 * (-half_c)) * (a * a + 1.0)
    root = jnp.where(one_minus_a2 > 0.0, one_minus_a2 * lax.rsqrt(one_minus_a2), 0.0)
    b = root * (i2 * (0.5 * u))
    return a, b


def _gelu_tanh(x):
    c = 0.7978845608028654
    t = jnp.tanh(x * ((x * x) * (0.044715 * c) + c))
    hx = 0.5 * x
    return hx * t + hx


def _silu(x):
    hx = 0.5 * x
    return hx * jnp.tanh(hx) + hx


def _slab_scratch(tile, w):
    return pltpu.VMEM((w // LANES, tile, LANES), F32)


def _lockstep_blocks(tile):
    seg = tile // SUBLANES
    return [(s, q) for s in range(SUBLANES) for q in range(seg // SUBLANES)], seg


def _to_lockstep(v, slab_ref):
    tt, w = v.shape
    blocks, seg = _lockstep_blocks(tt)
    for j in range(w // LANES):
        for s, q in blocks:
            r0 = s * seg + q * SUBLANES
            slab_ref[j, pl.ds(q * SUBLANES * SUBLANES + s, SUBLANES, stride=SUBLANES), :] = (
                v[r0:r0 + SUBLANES, j * LANES:(j + 1) * LANES])
    return jnp.concatenate([slab_ref[j] for j in range(w // LANES)], axis=1)


def _from_lockstep(slab_ref):
    n_lane, tt, _ = slab_ref.shape
    blocks, _ = _lockstep_blocks(tt)
    return jnp.concatenate(
        [jnp.concatenate(
            [slab_ref[j, pl.ds(q * SUBLANES * SUBLANES + s, SUBLANES, stride=SUBLANES), :]
             for s, q in blocks], axis=0) for j in range(n_lane)], axis=1)


def _scan_lockstep(a_ref, b_ref, carry_ref, reverse, emit):
    tt, w = a_ref.shape
    seg = tt // SUBLANES

    def rows(ref, i):
        return ref[pl.ds(pl.multiple_of(i * SUBLANES, SUBLANES), SUBLANES), :]

    def step_of(i):
        return seg - 1 - i if reverse else i

    def totals(i, carry):
        prod, acc = carry
        av = rows(a_ref, step_of(i))
        return av * prod, av * acc + rows(b_ref, step_of(i))

    init = (jnp.ones((SUBLANES, w), F32), jnp.zeros((SUBLANES, w), F32))
    prod, acc = lax.fori_loop(0, seg, totals, init, unroll=2)

    state = carry_ref[0:1, :]
    starts = [None] * SUBLANES
    for s in (range(SUBLANES - 1, -1, -1) if reverse else range(SUBLANES)):
        starts[s] = state
        state = prod[s:s + 1, :] * state + acc[s:s + 1, :]
    carry_ref[...] = jnp.broadcast_to(state, carry_ref.shape)

    def apply(i, h):
        t = step_of(i)
        h = rows(a_ref, t) * h + rows(b_ref, t)
        emit(t, h)
        return h

    lax.fori_loop(0, seg, apply, jnp.concatenate(starts, axis=0), unroll=2)


def _rglru_bwd_kernel(xp_ref, x_ref, xn_ref, mod_ref, g_ref, wx_ref, cw_ref, cb_ref,
                      wr_ref, br_ref, wi_ref, bi_ref, lam_ref, h0_ref,
                      u_ref, hb_ref, cout_ref, a_ref, b_ref, slab_ref, carry_ref):
    k = pl.program_id(0)
    n_tiles = pl.num_programs(0)
    tt = x_ref.shape[0]

    @pl.when(k == 0)
    def _():
        carry_ref[...] = h0_ref[...]

    xe = _with_halo(xp_ref, x_ref, xn_ref)
    he = _norm_mod(xe, g_ref[...], mod_ref[0:1, :], mod_ref[1:2, :]).astype(BF16)
    pre = _zero_outside(_dot(he, wx_ref[...]), SEQ_HALO, n_tiles - 1 - k, n_tiles)
    lo, hi = SEQ_HALO, SEQ_HALO + tt
    u = (cb_ref[...]
         + cw_ref[0:1, :] * _shift_rows(pre, 2)[lo:hi]
         + cw_ref[1:2, :] * _shift_rows(pre, 1)[lo:hi]
         + cw_ref[2:3, :] * pre[lo:hi]
         + cw_ref[3:4, :] * _shift_rows(pre, -1)[lo:hi])
    u = _to_lockstep(u, slab_ref)
    u_ref[...] = u
    a, b = _rglru_coeffs(u, wr_ref, br_ref, wi_ref, bi_ref, lam_ref)
    a_ref[...] = a
    b_ref[...] = b

    def emit(i, h):
        hb_ref[pl.ds(pl.multiple_of(i * SUBLANES, SUBLANES), SUBLANES), :] = h

    _scan_lockstep(a_ref, b_ref, carry_ref, True, emit)
    cout_ref[...] = carry_ref[...]


def _rglru_fwd_kernel(x_ref, u_ref, hb_ref, mod_ref, g_ref, wg_ref,
                      wr_ref, br_ref, wi_ref, bi_ref, lam_ref, wout_ref, h0_ref,
                      o_ref, cout_ref, a_ref, b_ref, slab_ref, carry_ref):
    @pl.when(pl.program_id(0) == 0)
    def _():
        carry_ref[...] = h0_ref[...]

    x = x_ref[...]
    h = _norm_mod(x, g_ref[...], mod_ref[0:1, :], mod_ref[1:2, :]).astype(BF16)
    gate = _gelu_tanh(_dot(h, wg_ref[...]))
    a, b = _rglru_coeffs(u_ref[...], wr_ref, br_ref, wi_ref, bi_ref, lam_ref)
    a_ref[...] = a
    b_ref[...] = b

    def emit(i, hf):
        r0 = pl.multiple_of(i * SUBLANES, SUBLANES)
        both = hf + hb_ref[pl.ds(r0, SUBLANES), :]
        for j in range(slab_ref.shape[0]):
            slab_ref[j, pl.ds(r0, SUBLANES), :] = both[:, j * LANES:(j + 1) * LANES]

    _scan_lockstep(a_ref, b_ref, carry_ref, False, emit)
    cout_ref[...] = carry_ref[...]
    mixed = (gate * _from_lockstep(slab_ref)).astype(BF16)
    o_ref[...] = x + mod_ref[2:3, :] * _dot(mixed, wout_ref[...])


def _gate_specs(w):
    hb = w // LRU_HEADS
    return [
        _const_spec((LRU_HEADS, hb, hb)),
        _const_spec((1, w)),
        _const_spec((LRU_HEADS, hb, hb)),
        _const_spec((1, w)),
        _const_spec((1, w)),
    ]


def _rglru_bwd(x, mod, gain, w_x, conv_w, conv_b, gates, h0, *, tile):
    t, d = x.shape
    w = w_x.shape[1]
    n_tiles = t // tile
    rev = lambda k: n_tiles - 1 - k
    prev, nxt = _halo_maps(tile, SEQ_HALO, t, rev)
    row = lambda k: (rev(k), 0)
    return pl.pallas_call(
        _rglru_bwd_kernel,
        grid=(n_tiles,),
        in_specs=[
            pl.BlockSpec((SEQ_HALO, d), prev),
            pl.BlockSpec((tile, d), row),
            pl.BlockSpec((SEQ_HALO, d), nxt),
            _const_spec((MOD_ROWS, d)),
            _const_spec((1, d)),
            _const_spec((d, w)),
            _const_spec((4, w)),
            _const_spec((1, w)),
            *_gate_specs(w),
            _const_spec((SUBLANES, w)),
        ],
        out_specs=[
            pl.BlockSpec((tile, w), row),
            pl.BlockSpec((tile, w), row),
            pl.BlockSpec((SUBLANES, w), lambda k: (0, 0)),
        ],
        out_shape=[
            jax.ShapeDtypeStruct((t, w), F32),
            jax.ShapeDtypeStruct((t, w), F32),
            jax.ShapeDtypeStruct((SUBLANES, w), F32),
        ],
        scratch_shapes=[
            pltpu.VMEM((tile, w), F32),
            pltpu.VMEM((tile, w), F32),
            _slab_scratch(tile, w),
            pltpu.VMEM((SUBLANES, w), F32),
        ],
        compiler_params=_params(1),
        name="rglru_bwd",
    )(x, x, x, mod, gain, w_x, conv_w, conv_b, *gates, h0)


def _rglru_fwd(x, u, hb, mod, gain, w_gate, gates, w_out, h0, *, tile):
    t, d = x.shape
    w = w_gate.shape[1]
    row = lambda k: (k, 0)
    return pl.pallas_call(
        _rglru_fwd_kernel,
        grid=(t // tile,),
        in_specs=[
            pl.BlockSpec((tile, d), row),
            pl.BlockSpec((tile, w), row),
            pl.BlockSpec((tile, w), row),
            _const_spec((MOD_ROWS, d)),
            _const_spec((1, d)),
            _const_spec((d, w)),
            *_gate_specs(w),
            _const_spec((w, d)),
            _const_spec((SUBLANES, w)),
        ],
        out_specs=[
            pl.BlockSpec((tile, d), row),
            pl.BlockSpec((SUBLANES, w), lambda k: (0, 0)),
        ],
        out_shape=[
            jax.ShapeDtypeStruct((t, d), F32),
            jax.ShapeDtypeStruct((SUBLANES, w), F32),
        ],
        scratch_shapes=[
            pltpu.VMEM((tile, w), F32),
            pltpu.VMEM((tile, w), F32),
            _slab_scratch(tile, w),
            pltpu.VMEM((SUBLANES, w), F32),
        ],
        compiler_params=_params(1),
        name="rglru_fwd",
    )(x, u, hb, mod, gain, w_gate, *gates, w_out, h0)


def kernel(x, c, ctx, c_ctx, ada_w, ada_b, norm_mix_g, norm_ffn_g, a_w_x, a_w_gate, a_conv_w, a_conv_b, a_w_r, a_b_r, a_w_i, a_b_i, a_lambda, a_w_out, b_w_in, b_conv_w, b_conv_b, b_w_out, f_w_up, f_conv_w, f_conv_b, f_w_down, final_g):
    bsz, seq, d = x.shape
    ctx_len = ctx.shape[1]
    depth = ada_w.shape[0]
    n_mixers = 2
    assert bsz == 1 and seq % TOKEN_TILE == 0 and TOKEN_TILE % GRID_W == 0
    lat_tile, ctx_tile = TOKEN_TILE, ctx_len

    cvecs = jnp.zeros((MOD_ROWS, d), F32).at[0].set(c[0]).at[1].set(c_ctx)
    mods = _adaln_all(cvecs, ada_w, ada_b).reshape(depth, MOD_ROWS, N_MOD, d)
    mods = jnp.pad(mods[:, :2], ((0, 0), (0, 0), (0, MOD_ROWS - N_MOD), (0, 0)))

    row = lambda v: v.reshape(1, -1)
    xl, xc = x[0], ctx[0]
    zero_state = jnp.zeros((SUBLANES, a_w_x.shape[2]), F32)
    for l in range(depth):
        j = l // n_mixers
        ctx_needed = any(k % n_mixers == 0 for k in range(l + 1, depth))
        mod_l, mod_c = mods[l, 0], mods[l, 1]
        g_mix = row(norm_mix_g[l])
        if l % n_mixers == 0:
            w_x, w_gate, w_out = (a_w_x[j].astype(BF16), a_w_gate[j].astype(BF16),
                                  a_w_out[j].astype(BF16))
            conv_w, conv_b = a_conv_w[j], row(a_conv_b[j])
            gates = [(a_w_r[j, dr].astype(BF16), row(a_b_r[j, dr]), a_w_i[j, dr].astype(BF16),
                      row(a_b_i[j, dr]), row(a_lambda[j, dr])) for dr in range(2)]
            u_c, hb_c, end_b = _rglru_bwd(xc, mod_c, g_mix, w_x, conv_w, conv_b, gates[1],
                                          zero_state, tile=ctx_tile)
            xc_new, end_f = _rglru_fwd(xc, u_c, hb_c, mod_c, g_mix, w_gate, gates[0], w_out,
                                       zero_state, tile=ctx_tile)
            u_l, hb_l, _ = _rglru_bwd(xl, mod_l, g_mix, w_x, conv_w, conv_b, gates[1],
                                      end_b, tile=lat_tile)
            xl, _ = _rglru_fwd(xl, u_l, hb_l, mod_l, g_mix, w_gate, gates[0], w_out,
                               end_f, tile=lat_tile)
            if ctx_needed:
                xc = xc_new
        else:
            w_in, w_out = b_w_in[j].astype(BF16), b_w_out[j].astype(BF16)
            conv_w, conv_b = b_conv_w[j], row(b_conv_b[j])
            xl = _shortconv(xl, mod_l, g_mix, w_in, conv_w, conv_b, w_out,
                            tile=lat_tile, shift=GRID_W)
            if ctx_needed:
                xc = _shortconv(xc, mod_c, g_mix, w_in, conv_w, conv_b, w_out,
                                tile=ctx_tile, shift=1)
        g_ffn = row(norm_ffn_g[l])
        w_up, w_down = f_w_up[l].astype(BF16), f_w_down[l].astype(BF16)
        conv_w, conv_b = f_conv_w[l], row(f_conv_b[l])
        xl = _conv_ffn(xl, mod_l, g_ffn, w_up, conv_w, conv_b, w_down, row(final_g),
                       tile=FFN_TILE, period=GRID_W, final_norm=(l == depth - 1))
        if ctx_needed:
            xc = _conv_ffn(xc, mod_c, g_ffn, w_up, conv_w, conv_b, w_down, row(final_g),
                           tile=ctx_tile, period=ctx_len, final_norm=False)
    return xl[None]
```

```python
import functools

import jax
import jax.numpy as jnp
from jax import lax
from jax.experimental import pallas as pl
from jax.experimental.pallas import tpu as pltpu

F32 = jnp.float32
BF16 = jnp.bfloat16

EPS = 1e-6
LRU_C = 8.0
LRU_HEADS = 4
GRID_W = 64
N_MOD = 6
MOD_ROWS = 8
SUBLANES = 8
LANES = 128
SEQ_HALO = 8
TOKEN_TILE = 512
RG_TILE = 256
FFN_TILE = 512
LOG2_E = 1.4426950408889634
FF_CHUNK = 256
ADA_COLS = 1536
VMEM_LIMIT = 56 * 1024 * 1024


def _dot(a, b):
    return jnp.dot(a, b, preferred_element_type=F32)


def _norm_mod(x, g, shift, scale):
    ms = jnp.mean(x * x, axis=-1, keepdims=True)
    return (x * lax.rsqrt(ms + EPS)) * (g * (1.0 + scale)) + shift


def _shift_rows(x, s):
    n = x.shape[0]
    return pltpu.roll(x, s % n, 0)


def _const_spec(shape):
    zeros = (0,) * len(shape)
    return pl.BlockSpec(shape, lambda *_: zeros, pipeline_mode=pl.Buffered(1))


def _params(n_axes):
    return pltpu.CompilerParams(
        dimension_semantics=("arbitrary",) * n_axes, vmem_limit_bytes=VMEM_LIMIT)


def _ada_kernel(c_ref, w_ref, b_ref, o_ref):
    cv = c_ref[...]
    s = jax.nn.silu(cv).astype(BF16)
    o_ref[0] = _dot(s, w_ref[0].astype(BF16)) + b_ref[0]


def _adaln_all(cvecs, ada_w, ada_b):
    depth, d, n = ada_w.shape
    return pl.pallas_call(
        _ada_kernel,
        grid=(depth, n // ADA_COLS),
        in_specs=[
            pl.BlockSpec((MOD_ROWS, d), lambda l, c: (0, 0)),
            pl.BlockSpec((1, d, ADA_COLS), lambda l, c: (l, 0, c)),
            pl.BlockSpec((1, 1, ADA_COLS), lambda l, c: (l, 0, c)),
        ],
        out_specs=pl.BlockSpec((1, MOD_ROWS, ADA_COLS), lambda l, c: (l, 0, c)),
        out_shape=jax.ShapeDtypeStruct((depth, MOD_ROWS, n), F32),
        compiler_params=_params(2),
        name="adaln",
    )(cvecs, ada_w, ada_b.reshape(depth, 1, n))


def _drain(*stages, head_start=0):
    results = [None] * len(stages)
    live = dict(enumerate(stages))
    for _ in range(head_start):
        next(stages[0])
    while live:
        for idx in list(live):
            try:
                next(live[idx])
            except StopIteration as stop:
                results[idx] = stop.value
                del live[idx]
    return results


def _ffn_up(x, mod_ref, g_ref, wup_ref, cw_ref, cb_ref, act_ref, *, period):
    tt = x.shape[0]
    d_ff = act_ref.shape[1]
    h = _norm_mod(x, g_ref[...], mod_ref[3:4, :], mod_ref[4:5, :]).astype(BF16)
    pos = lax.broadcasted_iota(jnp.int32, (tt, FF_CHUNK), 0) % period
    has_prev = pos != 0
    has_next = pos != period - 1
    yield
    for c in range(d_ff // FF_CHUNK):
        lo = c * FF_CHUNK
        g = _dot(h, wup_ref[:, lo:lo + FF_CHUNK])
        v = _dot(h, wup_ref[:, d_ff + lo:d_ff + lo + FF_CHUNK])
        g_prev = jnp.where(has_prev, _shift_rows(g, 1), 0.0)
        g_next = jnp.where(has_next, _shift_rows(g, -1), 0.0)
        cw = cw_ref[:, lo:lo + FF_CHUNK]
        cv = (cb_ref[:, lo:lo + FF_CHUNK] + cw[0:1, :] * g_prev
              + cw[1:2, :] * g + cw[2:3, :] * g_next)
        act_ref[:, lo:lo + FF_CHUNK] = (_silu(cv) * v).astype(BF16)
        yield


def _ffn_down(x, mod_ref, wdn_ref, fg_ref, act_ref, *, final_norm):
    y = x + mod_ref[5:6, :] * _dot(act_ref[...], wdn_ref[...])
    if final_norm:
        ms = jnp.mean(y * y, axis=-1, keepdims=True)
        y = y * lax.rsqrt(ms + EPS) * fg_ref[...]
    return y


def _ffn_kernel(x_ref, mod_ref, g_ref, wup_ref, cw_ref, cb_ref, wdn_ref, fg_ref, o_ref,
                act_ref, *, period, final_norm):
    x = x_ref[...]
    _drain(_ffn_up(x, mod_ref, g_ref, wup_ref, cw_ref, cb_ref, act_ref, period=period))
    o_ref[...] = _ffn_down(x, mod_ref, wdn_ref, fg_ref, act_ref, final_norm=final_norm)


def _conv_ffn(x, mod, gain, w_up, conv_w, conv_b, w_down, final_g, *, tile, period,
              final_norm):
    t, d = x.shape
    d_ff = w_down.shape[0]
    row = lambda k: (k, 0)
    return pl.pallas_call(
        functools.partial(_ffn_kernel, period=period, final_norm=final_norm),
        grid=(t // tile,),
        in_specs=[
            pl.BlockSpec((tile, d), row),
            _const_spec((MOD_ROWS, d)),
            _const_spec((1, d)),
            _const_spec((d, 2 * d_ff)),
            _const_spec((3, d_ff)),
            _const_spec((1, d_ff)),
            _const_spec((d_ff, d)),
            _const_spec((1, d)),
        ],
        out_specs=pl.BlockSpec((tile, d), row),
        out_shape=jax.ShapeDtypeStruct((t, d), F32),
        scratch_shapes=[pltpu.VMEM((tile, d_ff), BF16)],
        compiler_params=_params(1),
        name="conv_ffn",
    )(x, mod, gain, w_up, conv_w, conv_b, w_down, final_g)


def _halo_maps(tile, halo, t, tile_of_step):
    per_tile = tile // halo
    last = t // halo - 1
    prev = lambda k: (jnp.maximum(tile_of_step(k) * per_tile - 1, 0), 0)
    nxt = lambda k: (jnp.minimum((tile_of_step(k) + 1) * per_tile, last), 0)
    return prev, nxt


def _with_halo(xp_ref, x_ref, xn_ref):
    return jnp.concatenate([xp_ref[...], x_ref[...], xn_ref[...]], axis=0)


def _zero_outside(v, halo, tile_idx, n_tiles):
    n = v.shape[0]
    head = jnp.where(tile_idx > 0, v[:halo], 0.0)
    tail = jnp.where(tile_idx < n_tiles - 1, v[n - halo:], 0.0)
    return jnp.concatenate([head, v[halo:n - halo], tail], axis=0)


def _shortconv_kernel(xp_ref, x_ref, xn_ref, mod_ref, g_ref, win_ref, cw_ref, cb_ref,
                      wout_ref, o_ref, *, shift, halo):
    k = pl.program_id(0)
    x = x_ref[...]
    tt, d = x.shape
    xe = _with_halo(xp_ref, x_ref, xn_ref)
    he = _norm_mod(xe, g_ref[...], mod_ref[0:1, :], mod_ref[1:2, :]).astype(BF16)
    cv = _dot(he, win_ref[:, d:3 * d])
    z = _zero_outside(cv[:, :d] * cv[:, d:], halo, k, pl.num_programs(0))
    gate_b = _dot(he[halo:halo + tt], win_ref[:, 0:d])
    conv = (cb_ref[...]
            + cw_ref[0:1, :] * _shift_rows(z, shift)[halo:halo + tt]
            + cw_ref[1:2, :] * z[halo:halo + tt]
            + cw_ref[2:3, :] * _shift_rows(z, -shift)[halo:halo + tt])
    y = _dot((gate_b * conv).astype(BF16), wout_ref[...])
    o_ref[...] = x + mod_ref[2:3, :] * y


def _shortconv(x, mod, gain, w_in, conv_w, conv_b, w_out, *, tile, shift):
    t, d = x.shape
    halo = max(shift, SUBLANES)
    prev, nxt = _halo_maps(tile, halo, t, lambda k: k)
    row = lambda k: (k, 0)
    return pl.pallas_call(
        functools.partial(_shortconv_kernel, shift=shift, halo=halo),
        grid=(t // tile,),
        in_specs=[
            pl.BlockSpec((halo, d), prev),
            pl.BlockSpec((tile, d), row),
            pl.BlockSpec((halo, d), nxt),
            _const_spec((MOD_ROWS, d)),
            _const_spec((1, d)),
            _const_spec((d, 3 * d)),
            _const_spec((3, d)),
            _const_spec((1, d)),
            _const_spec((d, d)),
        ],
        out_specs=pl.BlockSpec((tile, d), row),
        out_shape=jax.ShapeDtypeStruct((t, d), F32),
        compiler_params=_params(1),
        name="shortconv",
    )(x, x, x, mod, gain, w_in, conv_w, conv_b, w_out)


def _rglru_coeffs(u, wr_ref, br_ref, wi_ref, bi_ref, lam_ref, a_ref, b_ref):
    hb = wr_ref.shape[1]
    for hd in range(LRU_HEADS):
        cols = slice(hd * hb, (hd + 1) * hb)
        uh = u[:, cols]
        ub = uh.astype(BF16)
        r2 = jnp.tanh(0.5 * (_dot(ub, wr_ref[hd]) + br_ref[:, cols])) + 1.0
        i2 = jnp.tanh(0.5 * (_dot(ub, wi_ref[hd]) + bi_ref[:, cols])) + 1.0
        lam = lam_ref[:, cols]
        log_sig = jnp.minimum(lam, 0.0) - jnp.log1p(jnp.exp(-jnp.abs(lam)))
        half_c = (0.5 * LRU_C) * log_sig
        a = jnp.exp2(r2 * (half_c * LOG2_E))
        one_minus_a2 = jnp.tanh(r2 * (-half_c)) * (a * a + 1.0)
        root = jnp.where(one_minus_a2 > 0.0, one_minus_a2 * lax.rsqrt(one_minus_a2), 0.0)
        a_ref[:, cols] = a
        b_ref[:, cols] = root * (i2 * (0.5 * uh))
        yield


def _gelu_tanh(x):
    c = 0.7978845608028654
    t = jnp.tanh(x * ((x * x) * (0.044715 * c) + c))
    hx = 0.5 * x
    return hx * t + hx


def _silu(x):
    hx = 0.5 * x
    return hx * jnp.tanh(hx) + hx


def _slab_scratch(tile, w):
    return pltpu.VMEM((w // LANES, tile, LANES), F32)


def _lockstep_blocks(tile):
    seg = tile // SUBLANES
    return [(s, q) for s in range(SUBLANES) for q in range(seg // SUBLANES)], seg


def _to_lockstep(v, slab_ref):
    tt, w = v.shape
    blocks, seg = _lockstep_blocks(tt)
    for j in range(w // LANES):
        for s, q in blocks:
            r0 = s * seg + q * SUBLANES
            slab_ref[j, pl.ds(q * SUBLANES * SUBLANES + s, SUBLANES, stride=SUBLANES), :] = (
                v[r0:r0 + SUBLANES, j * LANES:(j + 1) * LANES])
    return jnp.concatenate([slab_ref[j] for j in range(w // LANES)], axis=1)


def _from_lockstep(slab_ref):
    n_lane, tt, _ = slab_ref.shape
    blocks, _ = _lockstep_blocks(tt)
    return jnp.concatenate(
        [jnp.concatenate(
            [slab_ref[j, pl.ds(q * SUBLANES * SUBLANES + s, SUBLANES, stride=SUBLANES), :]
             for s, q in blocks], axis=0) for j in range(n_lane)], axis=1)


def _step_rows(i):
    if isinstance(i, int):
        return pl.ds(i * SUBLANES, SUBLANES)
    return pl.ds(pl.multiple_of(i * SUBLANES, SUBLANES), SUBLANES)


def _scan_lockstep(a_ref, b_ref, carry_ref, reverse, emit, *, straight_line=False):
    tt, w = a_ref.shape
    seg = tt // SUBLANES

    def run(body, init):
        if not straight_line:
            return lax.fori_loop(0, seg, body, init, unroll=2)
        for i in range(seg):
            init = body(i, init)
        return init

    def rows(ref, i):
        return ref[_step_rows(i), :]

    def step_of(i):
        return seg - 1 - i if reverse else i

    def totals(i, carry):
        prod, acc = carry
        av = rows(a_ref, step_of(i))
        return av * prod, av * acc + rows(b_ref, step_of(i))

    init = (jnp.ones((SUBLANES, w), F32), jnp.zeros((SUBLANES, w), F32))
    prod, acc = run(totals, init)

    state = carry_ref[0:1, :]
    starts = [None] * SUBLANES
    for s in (range(SUBLANES - 1, -1, -1) if reverse else range(SUBLANES)):
        starts[s] = state
        state = prod[s:s + 1, :] * state + acc[s:s + 1, :]
    carry_ref[...] = jnp.broadcast_to(state, carry_ref.shape)

    def apply(i, h):
        t = step_of(i)
        h = rows(a_ref, t) * h + rows(b_ref, t)
        emit(t, h)
        return h

    run(apply, jnp.concatenate(starts, axis=0))


def _rglru_bwd_kernel(xp_ref, x_ref, xn_ref, mod_ref, g_ref, wx_ref, cw_ref, cb_ref,
                      wr_ref, br_ref, wi_ref, bi_ref, lam_ref, h0_ref,
                      u_ref, hb_ref, cout_ref, a_ref, b_ref, slab_ref, carry_ref):
    k = pl.program_id(0)
    n_tiles = pl.num_programs(0)
    tt = x_ref.shape[0]

    @pl.when(k == 0)
    def _():
        carry_ref[...] = h0_ref[...]

    xe = _with_halo(xp_ref, x_ref, xn_ref)
    he = _norm_mod(xe, g_ref[...], mod_ref[0:1, :], mod_ref[1:2, :]).astype(BF16)
    pre = _zero_outside(_dot(he, wx_ref[...]), SEQ_HALO, n_tiles - 1 - k, n_tiles)
    lo, hi = SEQ_HALO, SEQ_HALO + tt
    u = (cb_ref[...]
         + cw_ref[0:1, :] * _shift_rows(pre, 2)[lo:hi]
         + cw_ref[1:2, :] * _shift_rows(pre, 1)[lo:hi]
         + cw_ref[2:3, :] * pre[lo:hi]
         + cw_ref[3:4, :] * _shift_rows(pre, -1)[lo:hi])
    u = _to_lockstep(u, slab_ref)
    u_ref[...] = u
    _drain(_rglru_coeffs(u, wr_ref, br_ref, wi_ref, bi_ref, lam_ref, a_ref, b_ref))

    def emit(i, h):
        hb_ref[_step_rows(i), :] = h

    _scan_lockstep(a_ref, b_ref, carry_ref, True, emit)
    cout_ref[...] = carry_ref[...]


def _rglru_fwd_pre(x, u_ref, mod_ref, g_ref, wg_ref, wr_ref, br_ref, wi_ref, bi_ref, lam_ref,
                   a_ref, b_ref):
    h = _norm_mod(x, g_ref[...], mod_ref[0:1, :], mod_ref[1:2, :]).astype(BF16)
    yield
    w = wg_ref.shape[1]
    blk = w // LRU_HEADS
    parts = []
    for c in range(LRU_HEADS):
        parts.append(_gelu_tanh(_dot(h, wg_ref[:, c * blk:(c + 1) * blk])))
        yield
    yield from _rglru_coeffs(u_ref[...], wr_ref, br_ref, wi_ref, bi_ref, lam_ref, a_ref, b_ref)
    return jnp.concatenate(parts, axis=-1)


def _rglru_fwd_scan(hb_ref, a_ref, b_ref, slab_ref, carry_ref, *, straight_line=False):
    def emit(i, hf):
        both = hf + hb_ref[_step_rows(i), :]
        for j in range(slab_ref.shape[0]):
            slab_ref[j, _step_rows(i), :] = both[:, j * LANES:(j + 1) * LANES]

    _scan_lockstep(a_ref, b_ref, carry_ref, False, emit, straight_line=straight_line)


def _rglru_fwd_post(x, gate, mod_ref, wout_ref, slab_ref):
    mixed = (gate * _from_lockstep(slab_ref)).astype(BF16)
    return x + mod_ref[2:3, :] * _dot(mixed, wout_ref[...])


def _rglru_fwd_kernel(x_ref, u_ref, hb_ref, mod_ref, g_ref, wg_ref,
                      wr_ref, br_ref, wi_ref, bi_ref, lam_ref, wout_ref, h0_ref,
                      o_ref, cout_ref, a_ref, b_ref, slab_ref, carry_ref):
    @pl.when(pl.program_id(0) == 0)
    def _():
        carry_ref[...] = h0_ref[...]

    x = x_ref[...]
    gate, = _drain(_rglru_fwd_pre(x, u_ref, mod_ref, g_ref, wg_ref, wr_ref, br_ref, wi_ref,
                                  bi_ref, lam_ref, a_ref, b_ref))
    _rglru_fwd_scan(hb_ref, a_ref, b_ref, slab_ref, carry_ref)
    cout_ref[...] = carry_ref[...]
    o_ref[...] = _rglru_fwd_post(x, gate, mod_ref, wout_ref, slab_ref)


def _rglru_fwd_ffn_kernel(x_ref, u_ref, hb_ref, mod_ref, gmix_ref, wg_ref,
                          wr_ref, br_ref, wi_ref, bi_ref, lam_ref, wout_ref, h0_ref,
                          gffn_ref, wup_ref, cw_ref, cb_ref, wdn_ref, fg_ref,
                          o_ref, a_ref, b_ref, slab_ref, carry_ref, xmid_ref, act_ref, *,
                          period, final_norm):
    @pl.when(pl.program_id(0) == 0)
    def _():
        carry_ref[...] = h0_ref[...]
        xmid_ref[...] = jnp.zeros_like(xmid_ref)

    x_prev, x_cur = xmid_ref[...], x_ref[...]
    _, gate = _drain(
        _ffn_up(x_prev, mod_ref, gffn_ref, wup_ref, cw_ref, cb_ref, act_ref, period=period),
        _rglru_fwd_pre(x_cur, u_ref, mod_ref, gmix_ref, wg_ref, wr_ref, br_ref, wi_ref, bi_ref,
                       lam_ref, a_ref, b_ref),
        head_start=2)
    _rglru_fwd_scan(hb_ref, a_ref, b_ref, slab_ref, carry_ref, straight_line=True)
    o_ref[...] = _ffn_down(x_prev, mod_ref, wdn_ref, fg_ref, act_ref, final_norm=final_norm)
    xmid_ref[...] = _rglru_fwd_post(x_cur, gate, mod_ref, wout_ref, slab_ref)


def _gate_specs(w):
    hb = w // LRU_HEADS
    return [
        _const_spec((LRU_HEADS, hb, hb)),
        _const_spec((1, w)),
        _const_spec((LRU_HEADS, hb, hb)),
        _const_spec((1, w)),
        _const_spec((1, w)),
    ]


def _rglru_bwd(x, mod, gain, w_x, conv_w, conv_b, gates, h0, *, tile):
    t, d = x.shape
    w = w_x.shape[1]
    n_tiles = t // tile
    rev = lambda k: n_tiles - 1 - k
    prev, nxt = _halo_maps(tile, SEQ_HALO, t, rev)
    row = lambda k: (rev(k), 0)
    return pl.pallas_call(
        _rglru_bwd_kernel,
        grid=(n_tiles,),
        in_specs=[
            pl.BlockSpec((SEQ_HALO, d), prev),
            pl.BlockSpec((tile, d), row),
            pl.BlockSpec((SEQ_HALO, d), nxt),
            _const_spec((MOD_ROWS, d)),
            _const_spec((1, d)),
            _const_spec((d, w)),
            _const_spec((4, w)),
            _const_spec((1, w)),
            *_gate_specs(w),
            _const_spec((SUBLANES, w)),
        ],
        out_specs=[
            pl.BlockSpec((tile, w), row),
            pl.BlockSpec((tile, w), row),
            pl.BlockSpec((SUBLANES, w), lambda k: (0, 0)),
        ],
        out_shape=[
            jax.ShapeDtypeStruct((t, w), F32),
            jax.ShapeDtypeStruct((t, w), F32),
            jax.ShapeDtypeStruct((SUBLANES, w), F32),
        ],
        scratch_shapes=[
            pltpu.VMEM((tile, w), F32),
            pltpu.VMEM((tile, w), F32),
            _slab_scratch(tile, w),
            pltpu.VMEM((SUBLANES, w), F32),
        ],
        compiler_params=_params(1),
        name="rglru_bwd",
    )(x, x, x, mod, gain, w_x, conv_w, conv_b, *gates, h0)


def _rglru_fwd(x, u, hb, mod, gain, w_gate, gates, w_out, h0, *, tile):
    t, d = x.shape
    w = w_gate.shape[1]
    row = lambda k: (k, 0)
    return pl.pallas_call(
        _rglru_fwd_kernel,
        grid=(t // tile,),
        in_specs=[
            pl.BlockSpec((tile, d), row),
            pl.BlockSpec((tile, w), row),
            pl.BlockSpec((tile, w), row),
            _const_spec((MOD_ROWS, d)),
            _const_spec((1, d)),
            _const_spec((d, w)),
            *_gate_specs(w),
            _const_spec((w, d)),
            _const_spec((SUBLANES, w)),
        ],
        out_specs=[
            pl.BlockSpec((tile, d), row),
            pl.BlockSpec((SUBLANES, w), lambda k: (0, 0)),
        ],
        out_shape=[
            jax.ShapeDtypeStruct((t, d), F32),
            jax.ShapeDtypeStruct((SUBLANES, w), F32),
        ],
        scratch_shapes=[
            pltpu.VMEM((tile, w), F32),
            pltpu.VMEM((tile, w), F32),
            _slab_scratch(tile, w),
            pltpu.VMEM((SUBLANES, w), F32),
        ],
        compiler_params=_params(1),
        name="rglru_fwd",
    )(x, u, hb, mod, gain, w_gate, *gates, w_out, h0)


def _rglru_fwd_ffn(x, u, hb, mod, g_mix, w_gate, gates, w_out, h0, g_ffn, w_up, conv_w, conv_b,
                   w_down, final_g, *, tile, period, final_norm):
    t, d = x.shape
    w = w_gate.shape[1]
    d_ff = w_down.shape[0]
    n_tiles = t // tile
    cur = lambda k: (jnp.minimum(k, n_tiles - 1), 0)
    done = lambda k: (jnp.maximum(k - 1, 0), 0)
    return pl.pallas_call(
        functools.partial(_rglru_fwd_ffn_kernel, period=period, final_norm=final_norm),
        grid=(n_tiles + 1,),
        in_specs=[
            pl.BlockSpec((tile, d), cur),
            pl.BlockSpec((tile, w), cur),
            pl.BlockSpec((tile, w), cur),
            _const_spec((MOD_ROWS, d)),
            _const_spec((1, d)),
            _const_spec((d, w)),
            *_gate_specs(w),
            _const_spec((w, d)),
            _const_spec((SUBLANES, w)),
            _const_spec((1, d)),
            _const_spec((d, 2 * d_ff)),
            _const_spec((3, d_ff)),
            _const_spec((1, d_ff)),
            _const_spec((d_ff, d)),
            _const_spec((1, d)),
        ],
        out_specs=pl.BlockSpec((tile, d), done),
        out_shape=jax.ShapeDtypeStruct((t, d), F32),
        scratch_shapes=[
            pltpu.VMEM((tile, w), F32),
            pltpu.VMEM((tile, w), F32),
            _slab_scratch(tile, w),
            pltpu.VMEM((SUBLANES, w), F32),
            pltpu.VMEM((tile, d), F32),
            pltpu.VMEM((tile, d_ff), BF16),
        ],
        compiler_params=_params(1),
        name="rglru_fwd_ffn",
    )(x, u, hb, mod, g_mix, w_gate, *gates, w_out, h0, g_ffn, w_up, conv_w, conv_b, w_down,
      final_g)


def kernel(x, c, ctx, c_ctx, ada_w, ada_b, norm_mix_g, norm_ffn_g, a_w_x, a_w_gate, a_conv_w, a_conv_b, a_w_r, a_b_r, a_w_i, a_b_i, a_lambda, a_w_out, b_w_in, b_conv_w, b_conv_b, b_w_out, f_w_up, f_conv_w, f_conv_b, f_w_down, final_g):
    bsz, seq, d = x.shape
    ctx_len = ctx.shape[1]
    depth = ada_w.shape[0]
    n_mixers = 2
    assert bsz == 1 and seq % TOKEN_TILE == 0 and TOKEN_TILE % GRID_W == 0
    lat_tile, ctx_tile = TOKEN_TILE, ctx_len

    cvecs = jnp.zeros((MOD_ROWS, d), F32).at[0].set(c[0]).at[1].set(c_ctx)
    mods = _adaln_all(cvecs, ada_w, ada_b).reshape(depth, MOD_ROWS, N_MOD, d)
    mods = jnp.pad(mods[:, :2], ((0, 0), (0, 0), (0, MOD_ROWS - N_MOD), (0, 0)))

    row = lambda v: v.reshape(1, -1)
    xl, xc = x[0], ctx[0]
    zero_state = jnp.zeros((SUBLANES, a_w_x.shape[2]), F32)
    for l in range(depth):
        j = l // n_mixers
        ctx_needed = any(k % n_mixers == 0 for k in range(l + 1, depth))
        mod_l, mod_c = mods[l, 0], mods[l, 1]
        g_mix, g_ffn = row(norm_mix_g[l]), row(norm_ffn_g[l])
        ffn = (g_ffn, f_w_up[l].astype(BF16), f_conv_w[l], row(f_conv_b[l]),
               f_w_down[l].astype(BF16), row(final_g))
        final_norm = l == depth - 1
        if l % n_mixers == 0:
            w_x, w_gate, w_out = (a_w_x[j].astype(BF16), a_w_gate[j].astype(BF16),
                                  a_w_out[j].astype(BF16))
            conv_w, conv_b = a_conv_w[j], row(a_conv_b[j])
            gates = [(a_w_r[j, dr].astype(BF16), row(a_b_r[j, dr]), a_w_i[j, dr].astype(BF16),
                      row(a_b_i[j, dr]), row(a_lambda[j, dr])) for dr in range(2)]
            u_c, hb_c, end_b = _rglru_bwd(xc, mod_c, g_mix, w_x, conv_w, conv_b, gates[1],
                                          zero_state, tile=ctx_tile)
            xc_new, end_f = _rglru_fwd(xc, u_c, hb_c, mod_c, g_mix, w_gate, gates[0], w_out,
                                       zero_state, tile=ctx_tile)
            u_l, hb_l, _ = _rglru_bwd(xl, mod_l, g_mix, w_x, conv_w, conv_b, gates[1],
                                      end_b, tile=RG_TILE)
            xl = _rglru_fwd_ffn(xl, u_l, hb_l, mod_l, g_mix, w_gate, gates[0], w_out, end_f,
                                *ffn, tile=RG_TILE, period=GRID_W, final_norm=final_norm)
            if ctx_needed:
                xc = xc_new
        else:
            w_in, w_out = b_w_in[j].astype(BF16), b_w_out[j].astype(BF16)
            conv_w, conv_b = b_conv_w[j], row(b_conv_b[j])
            xl = _shortconv(xl, mod_l, g_mix, w_in, conv_w, conv_b, w_out,
                            tile=lat_tile, shift=GRID_W)
            xl = _conv_ffn(xl, mod_l, *ffn, tile=FFN_TILE, period=GRID_W,
                           final_norm=final_norm)
            if ctx_needed:
                xc = _shortconv(xc, mod_c, g_mix, w_in, conv_w, conv_b, w_out,
                                tile=ctx_tile, shift=1)
        if ctx_needed:
            xc = _conv_ffn(xc, mod_c, *ffn, tile=ctx_tile, period=ctx_len, final_norm=False)
    return xl[None]
```

```python
import functools

import jax
import jax.numpy as jnp
from jax import lax
from jax.experimental import pallas as pl
from jax.experimental.pallas import tpu as pltpu

F32 = jnp.float32
BF16 = jnp.bfloat16

EPS = 1e-6
LRU_C = 8.0
LRU_HEADS = 4
GRID_W = 64
N_MOD = 6
MOD_ROWS = 8
SUBLANES = 8
LANES = 128
SEQ_HALO = 8
TOKEN_TILE = 1024
RG_TILE = 512
FFN_TILE = 1024
LOG2_E = 1.4426950408889634
FF_CHUNK = 256
ADA_COLS = 1536
VMEM_LIMIT = 56 * 1024 * 1024


def _dot(a, b):
    return jnp.dot(a, b, preferred_element_type=F32)


def _norm_mod(x, g, shift, scale):
    ms = jnp.mean(x * x, axis=-1, keepdims=True)
    return (x * lax.rsqrt(ms + EPS)) * (g * (1.0 + scale)) + shift


def _shift_rows(x, s):
    n = x.shape[0]
    return pltpu.roll(x, s % n, 0)


def _const_spec(shape):
    zeros = (0,) * len(shape)
    return pl.BlockSpec(shape, lambda *_: zeros, pipeline_mode=pl.Buffered(1))


def _params(n_axes):
    return pltpu.CompilerParams(
        dimension_semantics=("arbitrary",) * n_axes, vmem_limit_bytes=VMEM_LIMIT)


def _ada_kernel(c_ref, w_ref, b_ref, o_ref):
    cv = c_ref[...]
    s = jax.nn.silu(cv).astype(BF16)
    o_ref[0] = _dot(s, w_ref[0].astype(BF16)) + b_ref[0]


def _adaln_all(cvecs, ada_w, ada_b):
    depth, d, n = ada_w.shape
    return pl.pallas_call(
        _ada_kernel,
        grid=(depth, n // ADA_COLS),
        in_specs=[
            pl.BlockSpec((MOD_ROWS, d), lambda l, c: (0, 0)),
            pl.BlockSpec((1, d, ADA_COLS), lambda l, c: (l, 0, c)),
            pl.BlockSpec((1, 1, ADA_COLS), lambda l, c: (l, 0, c)),
        ],
        out_specs=pl.BlockSpec((1, MOD_ROWS, ADA_COLS), lambda l, c: (l, 0, c)),
        out_shape=jax.ShapeDtypeStruct((depth, MOD_ROWS, n), F32),
        compiler_params=_params(2),
        name="adaln",
    )(cvecs, ada_w, ada_b.reshape(depth, 1, n))


def _drain(*stages, head_start=0):
    results = [None] * len(stages)
    live = dict(enumerate(stages))
    for _ in range(head_start):
        next(stages[0])
    while live:
        for idx in list(live):
            try:
                next(live[idx])
            except StopIteration as stop:
                results[idx] = stop.value
                del live[idx]
    return results


def _ffn_up(x, mod_ref, g_ref, wup_ref, cw_ref, cb_ref, act_ref, *, period):
    tt = x.shape[0]
    d_ff = act_ref.shape[1]
    h = _norm_mod(x, g_ref[...], mod_ref[3:4, :], mod_ref[4:5, :]).astype(BF16)
    pos = lax.broadcasted_iota(jnp.int32, (tt, FF_CHUNK), 0) % period
    has_prev = pos != 0
    has_next = pos != period - 1
    yield
    for c in range(d_ff // FF_CHUNK):
        lo = c * FF_CHUNK
        g = _dot(h, wup_ref[:, lo:lo + FF_CHUNK])
        v = _dot(h, wup_ref[:, d_ff + lo:d_ff + lo + FF_CHUNK])
        g_prev = jnp.where(has_prev, _shift_rows(g, 1), 0.0)
        g_next = jnp.where(has_next, _shift_rows(g, -1), 0.0)
        cw = cw_ref[:, lo:lo + FF_CHUNK]
        cv = (cb_ref[:, lo:lo + FF_CHUNK] + cw[0:1, :] * g_prev
              + cw[1:2, :] * g + cw[2:3, :] * g_next)
        act_ref[:, lo:lo + FF_CHUNK] = (_silu(cv) * v).astype(BF16)
        yield


def _ffn_down(x, mod_ref, wdn_ref, fg_ref, act_ref, *, final_norm):
    y = x + mod_ref[5:6, :] * _dot(act_ref[...], wdn_ref[...])
    if final_norm:
        ms = jnp.mean(y * y, axis=-1, keepdims=True)
        y = y * lax.rsqrt(ms + EPS) * fg_ref[...]
    return y


def _ffn_kernel(x_ref, mod_ref, g_ref, wup_ref, cw_ref, cb_ref, wdn_ref, fg_ref, o_ref,
                act_ref, *, period, final_norm):
    x = x_ref[...]
    _drain(_ffn_up(x, mod_ref, g_ref, wup_ref, cw_ref, cb_ref, act_ref, period=period))
    o_ref[...] = _ffn_down(x, mod_ref, wdn_ref, fg_ref, act_ref, final_norm=final_norm)


def _conv_ffn(x, mod, gain, w_up, conv_w, conv_b, w_down, final_g, *, tile, period,
              final_norm):
    t, d = x.shape
    d_ff = w_down.shape[0]
    row = lambda k: (k, 0)
    return pl.pallas_call(
        functools.partial(_ffn_kernel, period=period, final_norm=final_norm),
        grid=(t // tile,),
        in_specs=[
            pl.BlockSpec((tile, d), row),
            _const_spec((MOD_ROWS, d)),
            _const_spec((1, d)),
            _const_spec((d, 2 * d_ff)),
            _const_spec((3, d_ff)),
            _const_spec((1, d_ff)),
            _const_spec((d_ff, d)),
            _const_spec((1, d)),
        ],
        out_specs=pl.BlockSpec((tile, d), row),
        out_shape=jax.ShapeDtypeStruct((t, d), F32),
        scratch_shapes=[pltpu.VMEM((tile, d_ff), BF16)],
        compiler_params=_params(1),
        name="conv_ffn",
    )(x, mod, gain, w_up, conv_w, conv_b, w_down, final_g)


def _halo_maps(tile, halo, t, tile_of_step):
    per_tile = tile // halo
    last = t // halo - 1
    prev = lambda k: (jnp.maximum(tile_of_step(k) * per_tile - 1, 0), 0)
    nxt = lambda k: (jnp.minimum((tile_of_step(k) + 1) * per_tile, last), 0)
    return prev, nxt


def _with_halo(xp_ref, x_ref, xn_ref):
    return jnp.concatenate([xp_ref[...], x_ref[...], xn_ref[...]], axis=0)


def _zero_outside(v, halo, tile_idx, n_tiles):
    n = v.shape[0]
    head = jnp.where(tile_idx > 0, v[:halo], 0.0)
    tail = jnp.where(tile_idx < n_tiles - 1, v[n - halo:], 0.0)
    return jnp.concatenate([head, v[halo:n - halo], tail], axis=0)


def _shortconv_kernel(xp_ref, x_ref, xn_ref, mod_ref, g_ref, win_ref, cw_ref, cb_ref,
                      wout_ref, o_ref, *, shift, halo):
    k = pl.program_id(0)
    x = x_ref[...]
    tt, d = x.shape
    xe = _with_halo(xp_ref, x_ref, xn_ref)
    he = _norm_mod(xe, g_ref[...], mod_ref[0:1, :], mod_ref[1:2, :]).astype(BF16)
    cv = _dot(he, win_ref[:, d:3 * d])
    z = _zero_outside(cv[:, :d] * cv[:, d:], halo, k, pl.num_programs(0))
    gate_b = _dot(he[halo:halo + tt], win_ref[:, 0:d])
    conv = (cb_ref[...]
            + cw_ref[0:1, :] * _shift_rows(z, shift)[halo:halo + tt]
            + cw_ref[1:2, :] * z[halo:halo + tt]
            + cw_ref[2:3, :] * _shift_rows(z, -shift)[halo:halo + tt])
    y = _dot((gate_b * conv).astype(BF16), wout_ref[...])
    o_ref[...] = x + mod_ref[2:3, :] * y


def _shortconv(x, mod, gain, w_in, conv_w, conv_b, w_out, *, tile, shift):
    t, d = x.shape
    halo = max(shift, SUBLANES)
    prev, nxt = _halo_maps(tile, halo, t, lambda k: k)
    row = lambda k: (k, 0)
    return pl.pallas_call(
        functools.partial(_shortconv_kernel, shift=shift, halo=halo),
        grid=(t // tile,),
        in_specs=[
            pl.BlockSpec((halo, d), prev),
            pl.BlockSpec((tile, d), row),
            pl.BlockSpec((halo, d), nxt),
            _const_spec((MOD_ROWS, d)),
            _const_spec((1, d)),
            _const_spec((d, 3 * d)),
            _const_spec((3, d)),
            _const_spec((1, d)),
            _const_spec((d, d)),
        ],
        out_specs=pl.BlockSpec((tile, d), row),
        out_shape=jax.ShapeDtypeStruct((t, d), F32),
        compiler_params=_params(1),
        name="shortconv",
    )(x, x, x, mod, gain, w_in, conv_w, conv_b, w_out)


def _rglru_coeffs(u, wr_ref, br_ref, wi_ref, bi_ref, lam_ref, a_ref, b_ref):
    hb = wr_ref.shape[1]
    for hd in range(LRU_HEADS):
        cols = slice(hd * hb, (hd + 1) * hb)
        uh = u[:, cols]
        ub = uh.astype(BF16)
        r2 = jnp.tanh(0.5 * (_dot(ub, wr_ref[hd]) + br_ref[:, cols])) + 1.0
        i2 = jnp.tanh(0.5 * (_dot(ub, wi_ref[hd]) + bi_ref[:, cols])) + 1.0
        lam = lam_ref[:, cols]
        log_sig = jnp.minimum(lam, 0.0) - jnp.log1p(jnp.exp(-jnp.abs(lam)))
        half_c = (0.5 * LRU_C) * log_sig
        a = jnp.exp2(r2 * (half_c * LOG2_E))
        one_minus_a2 = jnp.tanh(r2 * (-half_c)) * (a * a + 1.0)
        root = jnp.where(one_minus_a2 > 0.0, one_minus_a2 * lax.rsqrt(one_minus_a2), 0.0)
        a_ref[:, cols] = a
        b_ref[:, cols] = root * (i2 * (0.5 * uh))
        yield


def _gelu_tanh(x):
    c = 0.7978845608028654
    t = jnp.tanh(x * ((x * x) * (0.044715 * c) + c))
    hx = 0.5 * x
    return hx * t + hx


def _silu(x):
    hx = 0.5 * x
    return hx * jnp.tanh(hx) + hx


def _slab_scratch(tile, w):
    return pltpu.VMEM((w // LANES, tile, LANES), F32)


def _lockstep_blocks(tile):
    seg = tile // SUBLANES
    return [(s, q) for s in range(SUBLANES) for q in range(seg // SUBLANES)], seg


def _to_lockstep(v, slab_ref):
    tt, w = v.shape
    blocks, seg = _lockstep_blocks(tt)
    for j in range(w // LANES):
        for s, q in blocks:
            r0 = s * seg + q * SUBLANES
            slab_ref[j, pl.ds(q * SUBLANES * SUBLANES + s, SUBLANES, stride=SUBLANES), :] = (
                v[r0:r0 + SUBLANES, j * LANES:(j + 1) * LANES])
    return jnp.concatenate([slab_ref[j] for j in range(w // LANES)], axis=1)


def _from_lockstep(slab_ref):
    n_lane, tt, _ = slab_ref.shape
    blocks, _ = _lockstep_blocks(tt)
    return jnp.concatenate(
        [jnp.concatenate(
            [slab_ref[j, pl.ds(q * SUBLANES * SUBLANES + s, SUBLANES, stride=SUBLANES), :]
             for s, q in blocks], axis=0) for j in range(n_lane)], axis=1)


def _step_rows(i):
    if isinstance(i, int):
        return pl.ds(i * SUBLANES, SUBLANES)
    return pl.ds(pl.multiple_of(i * SUBLANES, SUBLANES), SUBLANES)


def _scan_lockstep(a_ref, b_ref, carry_ref, reverse, emit, *, straight_line=False):
    tt, w = a_ref.shape
    seg = tt // SUBLANES

    def run(body, init):
        if not straight_line:
            return lax.fori_loop(0, seg, body, init, unroll=2)
        for i in range(seg):
            init = body(i, init)
        return init

    def rows(ref, i):
        return ref[_step_rows(i), :]

    def step_of(i):
        return seg - 1 - i if reverse else i

    def totals(i, carry):
        prod, acc = carry
        av = rows(a_ref, step_of(i))
        return av * prod, av * acc + rows(b_ref, step_of(i))

    init = (jnp.ones((SUBLANES, w), F32), jnp.zeros((SUBLANES, w), F32))
    prod, acc = run(totals, init)

    state = carry_ref[0:1, :]
    starts = [None] * SUBLANES
    for s in (range(SUBLANES - 1, -1, -1) if reverse else range(SUBLANES)):
        starts[s] = state
        state = prod[s:s + 1, :] * state + acc[s:s + 1, :]
    carry_ref[...] = jnp.broadcast_to(state, carry_ref.shape)

    def apply(i, h):
        t = step_of(i)
        h = rows(a_ref, t) * h + rows(b_ref, t)
        emit(t, h)
        return h

    run(apply, jnp.concatenate(starts, axis=0))


def _rglru_bwd_kernel(xp_ref, x_ref, xn_ref, mod_ref, g_ref, wx_ref, cw_ref, cb_ref,
                      wr_ref, br_ref, wi_ref, bi_ref, lam_ref, h0_ref,
                      u_ref, hb_ref, cout_ref, a_ref, b_ref, slab_ref, carry_ref):
    k = pl.program_id(0)
    n_tiles = pl.num_programs(0)
    tt = x_ref.shape[0]

    @pl.when(k == 0)
    def _():
        carry_ref[...] = h0_ref[...]

    xe = _with_halo(xp_ref, x_ref, xn_ref)
    he = _norm_mod(xe, g_ref[...], mod_ref[0:1, :], mod_ref[1:2, :]).astype(BF16)
    pre = _zero_outside(_dot(he, wx_ref[...]), SEQ_HALO, n_tiles - 1 - k, n_tiles)
    lo, hi = SEQ_HALO, SEQ_HALO + tt
    u = (cb_ref[...]
         + cw_ref[0:1, :] * _shift_rows(pre, 2)[lo:hi]
         + cw_ref[1:2, :] * _shift_rows(pre, 1)[lo:hi]
         + cw_ref[2:3, :] * pre[lo:hi]
         + cw_ref[3:4, :] * _shift_rows(pre, -1)[lo:hi])
    u = _to_lockstep(u, slab_ref)
    u_ref[...] = u
    _drain(_rglru_coeffs(u, wr_ref, br_ref, wi_ref, bi_ref, lam_ref, a_ref, b_ref))

    def emit(i, h):
        hb_ref[_step_rows(i), :] = h

    _scan_lockstep(a_ref, b_ref, carry_ref, True, emit)
    cout_ref[...] = carry_ref[...]


def _rglru_fwd_pre(x, u_ref, mod_ref, g_ref, wg_ref, wr_ref, br_ref, wi_ref, bi_ref, lam_ref,
                   a_ref, b_ref):
    h = _norm_mod(x, g_ref[...], mod_ref[0:1, :], mod_ref[1:2, :]).astype(BF16)
    yield
    w = wg_ref.shape[1]
    blk = w // LRU_HEADS
    parts = []
    for c in range(LRU_HEADS):
        parts.append(_gelu_tanh(_dot(h, wg_ref[:, c * blk:(c + 1) * blk])))
        yield
    yield from _rglru_coeffs(u_ref[...], wr_ref, br_ref, wi_ref, bi_ref, lam_ref, a_ref, b_ref)
    return jnp.concatenate(parts, axis=-1)


def _rglru_fwd_scan(hb_ref, a_ref, b_ref, slab_ref, carry_ref, *, straight_line=False):
    def emit(i, hf):
        both = hf + hb_ref[_step_rows(i), :]
        for j in range(slab_ref.shape[0]):
            slab_ref[j, _step_rows(i), :] = both[:, j * LANES:(j + 1) * LANES]

    _scan_lockstep(a_ref, b_ref, carry_ref, False, emit, straight_line=straight_line)


def _rglru_fwd_post(x, gate, mod_ref, wout_ref, slab_ref):
    mixed = (gate * _from_lockstep(slab_ref)).astype(BF16)
    return x + mod_ref[2:3, :] * _dot(mixed, wout_ref[...])


def _rglru_fwd_kernel(x_ref, u_ref, hb_ref, mod_ref, g_ref, wg_ref,
                      wr_ref, br_ref, wi_ref, bi_ref, lam_ref, wout_ref, h0_ref,
                      o_ref, cout_ref, a_ref, b_ref, slab_ref, carry_ref):
    @pl.when(pl.program_id(0) == 0)
    def _():
        carry_ref[...] = h0_ref[...]

    x = x_ref[...]
    gate, = _drain(_rglru_fwd_pre(x, u_ref, mod_ref, g_ref, wg_ref, wr_ref, br_ref, wi_ref,
                                  bi_ref, lam_ref, a_ref, b_ref))
    _rglru_fwd_scan(hb_ref, a_ref, b_ref, slab_ref, carry_ref)
    cout_ref[...] = carry_ref[...]
    o_ref[...] = _rglru_fwd_post(x, gate, mod_ref, wout_ref, slab_ref)


def _rglru_fwd_ffn_kernel(x_ref, u_ref, hb_ref, mod_ref, gmix_ref, wg_ref,
                          wr_ref, br_ref, wi_ref, bi_ref, lam_ref, wout_ref, h0_ref,
                          gffn_ref, wup_ref, cw_ref, cb_ref, wdn_ref, fg_ref,
                          o_ref, a_ref, b_ref, slab_ref, carry_ref, xmid_ref, act_ref, *,
                          period, final_norm):
    @pl.when(pl.program_id(0) == 0)
    def _():
        carry_ref[...] = h0_ref[...]
        xmid_ref[...] = jnp.zeros_like(xmid_ref)

    x_prev, x_cur = xmid_ref[...], x_ref[...]
    _, gate = _drain(
        _ffn_up(x_prev, mod_ref, gffn_ref, wup_ref, cw_ref, cb_ref, act_ref, period=period),
        _rglru_fwd_pre(x_cur, u_ref, mod_ref, gmix_ref, wg_ref, wr_ref, br_ref, wi_ref, bi_ref,
                       lam_ref, a_ref, b_ref),
        head_start=2)
    _rglru_fwd_scan(hb_ref, a_ref, b_ref, slab_ref, carry_ref, straight_line=True)
    o_ref[...] = _ffn_down(x_prev, mod_ref, wdn_ref, fg_ref, act_ref, final_norm=final_norm)
    xmid_ref[...] = _rglru_fwd_post(x_cur, gate, mod_ref, wout_ref, slab_ref)


def _gate_specs(w):
    hb = w // LRU_HEADS
    return [
        _const_spec((LRU_HEADS, hb, hb)),
        _const_spec((1, w)),
        _const_spec((LRU_HEADS, hb, hb)),
        _const_spec((1, w)),
        _const_spec((1, w)),
    ]


def _rglru_bwd(x, mod, gain, w_x, conv_w, conv_b, gates, h0, *, tile):
    t, d = x.shape
    w = w_x.shape[1]
    n_tiles = t // tile
    rev = lambda k: n_tiles - 1 - k
    prev, nxt = _halo_maps(tile, SEQ_HALO, t, rev)
    row = lambda k: (rev(k), 0)
    return pl.pallas_call(
        _rglru_bwd_kernel,
        grid=(n_tiles,),
        in_specs=[
            pl.BlockSpec((SEQ_HALO, d), prev),
            pl.BlockSpec((tile, d), row),
            pl.BlockSpec((SEQ_HALO, d), nxt),
            _const_spec((MOD_ROWS, d)),
            _const_spec((1, d)),
            _const_spec((d, w)),
            _const_spec((4, w)),
            _const_spec((1, w)),
            *_gate_specs(w),
            _const_spec((SUBLANES, w)),
        ],
        out_specs=[
            pl.BlockSpec((tile, w), row),
            pl.BlockSpec((tile, w), row),
            pl.BlockSpec((SUBLANES, w), lambda k: (0, 0)),
        ],
        out_shape=[
            jax.ShapeDtypeStruct((t, w), F32),
            jax.ShapeDtypeStruct((t, w), F32),
            jax.ShapeDtypeStruct((SUBLANES, w), F32),
        ],
        scratch_shapes=[
            pltpu.VMEM((tile, w), F32),
            pltpu.VMEM((tile, w), F32),
            _slab_scratch(tile, w),
            pltpu.VMEM((SUBLANES, w), F32),
        ],
        compiler_params=_params(1),
        name="rglru_bwd",
    )(x, x, x, mod, gain, w_x, conv_w, conv_b, *gates, h0)


def _rglru_fwd(x, u, hb, mod, gain, w_gate, gates, w_out, h0, *, tile):
    t, d = x.shape
    w = w_gate.shape[1]
    row = lambda k: (k, 0)
    return pl.pallas_call(
        _rglru_fwd_kernel,
        grid=(t // tile,),
        in_specs=[
            pl.BlockSpec((tile, d), row),
            pl.BlockSpec((tile, w), row),
            pl.BlockSpec((tile, w), row),
            _const_spec((MOD_ROWS, d)),
            _const_spec((1, d)),
            _const_spec((d, w)),
            *_gate_specs(w),
            _const_spec((w, d)),
            _const_spec((SUBLANES, w)),
        ],
        out_specs=[
            pl.BlockSpec((tile, d), row),
            pl.BlockSpec((SUBLANES, w), lambda k: (0, 0)),
        ],
        out_shape=[
            jax.ShapeDtypeStruct((t, d), F32),
            jax.ShapeDtypeStruct((SUBLANES, w), F32),
        ],
        scratch_shapes=[
            pltpu.VMEM((tile, w), F32),
            pltpu.VMEM((tile, w), F32),
            _slab_scratch(tile, w),
            pltpu.VMEM((SUBLANES, w), F32),
        ],
        compiler_params=_params(1),
        name="rglru_fwd",
    )(x, u, hb, mod, gain, w_gate, *gates, w_out, h0)


def _rglru_fwd_ffn(x, u, hb, mod, g_mix, w_gate, gates, w_out, h0, g_ffn, w_up, conv_w, conv_b,
                   w_down, final_g, *, tile, period, final_norm):
    t, d = x.shape
    w = w_gate.shape[1]
    d_ff = w_down.shape[0]
    n_tiles = t // tile
    cur = lambda k: (jnp.minimum(k, n_tiles - 1), 0)
    done = lambda k: (jnp.maximum(k - 1, 0), 0)
    return pl.pallas_call(
        functools.partial(_rglru_fwd_ffn_kernel, period=period, final_norm=final_norm),
        grid=(n_tiles + 1,),
        in_specs=[
            pl.BlockSpec((tile, d), cur),
            pl.BlockSpec((tile, w), cur),
            pl.BlockSpec((tile, w), cur),
            _const_spec((MOD_ROWS, d)),
            _const_spec((1, d)),
            _const_spec((d, w)),
            *_gate_specs(w),
            _const_spec((w, d)),
            _const_spec((SUBLANES, w)),
            _const_spec((1, d)),
            _const_spec((d, 2 * d_ff)),
            _const_spec((3, d_ff)),
            _const_spec((1, d_ff)),
            _const_spec((d_ff, d)),
            _const_spec((1, d)),
        ],
        out_specs=pl.BlockSpec((tile, d), done),
        out_shape=jax.ShapeDtypeStruct((t, d), F32),
        scratch_shapes=[
            pltpu.VMEM((tile, w), F32),
            pltpu.VMEM((tile, w), F32),
            _slab_scratch(tile, w),
            pltpu.VMEM((SUBLANES, w), F32),
            pltpu.VMEM((tile, d), F32),
            pltpu.VMEM((tile, d_ff), BF16),
        ],
        compiler_params=_params(1),
        name="rglru_fwd_ffn",
    )(x, u, hb, mod, g_mix, w_gate, *gates, w_out, h0, g_ffn, w_up, conv_w, conv_b, w_down,
      final_g)


def kernel(x, c, ctx, c_ctx, ada_w, ada_b, norm_mix_g, norm_ffn_g, a_w_x, a_w_gate, a_conv_w, a_conv_b, a_w_r, a_b_r, a_w_i, a_b_i, a_lambda, a_w_out, b_w_in, b_conv_w, b_conv_b, b_w_out, f_w_up, f_conv_w, f_conv_b, f_w_down, final_g):
    bsz, seq, d = x.shape
    ctx_len = ctx.shape[1]
    depth = ada_w.shape[0]
    n_mixers = 2
    assert bsz == 1 and seq % TOKEN_TILE == 0 and TOKEN_TILE % GRID_W == 0
    lat_tile, ctx_tile = TOKEN_TILE, ctx_len

    cvecs = jnp.zeros((MOD_ROWS, d), F32).at[0].set(c[0]).at[1].set(c_ctx)
    mods = _adaln_all(cvecs, ada_w, ada_b).reshape(depth, MOD_ROWS, N_MOD, d)
    mods = jnp.pad(mods[:, :2], ((0, 0), (0, 0), (0, MOD_ROWS - N_MOD), (0, 0)))

    row = lambda v: v.reshape(1, -1)
    xl, xc = x[0], ctx[0]
    zero_state = jnp.zeros((SUBLANES, a_w_x.shape[2]), F32)
    for l in range(depth):
        j = l // n_mixers
        ctx_needed = any(k % n_mixers == 0 for k in range(l + 1, depth))
        mod_l, mod_c = mods[l, 0], mods[l, 1]
        g_mix, g_ffn = row(norm_mix_g[l]), row(norm_ffn_g[l])
        ffn = (g_ffn, f_w_up[l].astype(BF16), f_conv_w[l], row(f_conv_b[l]),
               f_w_down[l].astype(BF16), row(final_g))
        final_norm = l == depth - 1
        if l % n_mixers == 0:
            w_x, w_gate, w_out = (a_w_x[j].astype(BF16), a_w_gate[j].astype(BF16),
                                  a_w_out[j].astype(BF16))
            conv_w, conv_b = a_conv_w[j], row(a_conv_b[j])
            gates = [(a_w_r[j, dr].astype(BF16), row(a_b_r[j, dr]), a_w_i[j, dr].astype(BF16),
                      row(a_b_i[j, dr]), row(a_lambda[j, dr])) for dr in range(2)]
            u_c, hb_c, end_b = _rglru_bwd(xc, mod_c, g_mix, w_x, conv_w, conv_b, gates[1],
                                          zero_state, tile=ctx_tile)
            xc_new, end_f = _rglru_fwd(xc, u_c, hb_c, mod_c, g_mix, w_gate, gates[0], w_out,
                                       zero_state, tile=ctx_tile)
            u_l, hb_l, _ = _rglru_bwd(xl, mod_l, g_mix, w_x, conv_w, conv_b, gates[1],
                                      end_b, tile=RG_TILE)
            xl = _rglru_fwd_ffn(xl, u_l, hb_l, mod_l, g_mix, w_gate, gates[0], w_out, end_f,
                                *ffn, tile=RG_TILE, period=GRID_W, final_norm=final_norm)
            if ctx_needed:
                xc = xc_new
        else:
            w_in, w_out = b_w_in[j].astype(BF16), b_w_out[j].astype(BF16)
            conv_w, conv_b = b_conv_w[j], row(b_conv_b[j])
            xl = _shortconv(xl, mod_l, g_mix, w_in, conv_w, conv_b, w_out,
                            tile=lat_tile, shift=GRID_W)
            xl = _conv_ffn(xl, mod_l, *ffn, tile=FFN_TILE, period=GRID_W,
                           final_norm=final_norm)
            if ctx_needed:
                xc = _shortconv(xc, mod_c, g_mix, w_in, conv_w, conv_b, w_out,
                                tile=ctx_tile, shift=1)
        if ctx_needed:
            xc = _conv_ffn(xc, mod_c, *ffn, tile=ctx_tile, period=ctx_len, final_norm=False)
    return xl[None]
```

```python
import functools

import jax
import jax.numpy as jnp
from jax import lax
from jax.experimental import pallas as pl
from jax.experimental.pallas import tpu as pltpu

F32 = jnp.float32
BF16 = jnp.bfloat16

EPS = 1e-6
LRU_C = 8.0
LRU_HEADS = 4
GRID_W = 64
N_MOD = 6
MOD_ROWS = 8
SUBLANES = 8
LANES = 128
SEQ_HALO = 8
TOKEN_TILE = 1024
RG_TILE = 512
FFN_TILE = 1024
LOG2_E = 1.4426950408889634
FF_CHUNK = 256
ADA_COLS = 1536
VMEM_LIMIT = 56 * 1024 * 1024


def _dot(a, b):
    return jnp.dot(a, b, preferred_element_type=F32)


def _norm_mod(x, g, shift, scale):
    ms = jnp.mean(x * x, axis=-1, keepdims=True)
    return (x * lax.rsqrt(ms + EPS)) * (g * (1.0 + scale)) + shift


def _shift_rows(x, s):
    n = x.shape[0]
    return pltpu.roll(x, s % n, 0)


def _pick(arr, *idx):
    return arr, idx


def _picked(picks):
    specs = []
    for arr, idx in picks:
        shape = arr.shape[len(idx):]
        at = idx + (0,) * len(shape)
        specs.append(pl.BlockSpec((None,) * len(idx) + shape, lambda *_, at=at: at,
                                  pipeline_mode=pl.Buffered(1)))
    return specs, [arr for arr, _ in picks]


def _params(n_axes):
    return pltpu.CompilerParams(
        dimension_semantics=("arbitrary",) * n_axes, vmem_limit_bytes=VMEM_LIMIT)


def _ada_kernel(c_ref, w_ref, b_ref, o_ref):
    cv = c_ref[...]
    s = jax.nn.silu(cv).astype(BF16)
    o_ref[0] = _dot(s, w_ref[0].astype(BF16)) + b_ref[0]


def _adaln_all(cvecs, ada_w, ada_b):
    depth, d, n = ada_w.shape
    return pl.pallas_call(
        _ada_kernel,
        grid=(depth, n // ADA_COLS),
        in_specs=[
            pl.BlockSpec((MOD_ROWS, d), lambda l, c: (0, 0)),
            pl.BlockSpec((1, d, ADA_COLS), lambda l, c: (l, 0, c)),
            pl.BlockSpec((1, 1, ADA_COLS), lambda l, c: (l, 0, c)),
        ],
        out_specs=pl.BlockSpec((1, MOD_ROWS, ADA_COLS), lambda l, c: (l, 0, c)),
        out_shape=jax.ShapeDtypeStruct((depth, MOD_ROWS, n), F32),
        compiler_params=_params(2),
        name="adaln",
    )(cvecs, ada_w, ada_b.reshape(depth, 1, n))


def _drain(*stages, head_start=0):
    results = [None] * len(stages)
    live = dict(enumerate(stages))
    for _ in range(head_start):
        next(stages[0])
    while live:
        for idx in list(live):
            try:
                next(live[idx])
            except StopIteration as stop:
                results[idx] = stop.value
                del live[idx]
    return results


def _ffn_in(x, mod_ref, g_ref):
    return _norm_mod(x, g_ref[...], mod_ref[3:4, :], mod_ref[4:5, :]).astype(BF16)


def _ffn_up(h, wup_ref, cw_ref, cb_ref, act_ref, *, period):
    tt = h.shape[0]
    d_ff = act_ref.shape[1]
    pos = lax.broadcasted_iota(jnp.int32, (tt, FF_CHUNK), 0) % period
    has_prev = pos != 0
    has_next = pos != period - 1
    for c in range(d_ff // FF_CHUNK):
        lo = c * FF_CHUNK
        g = _dot(h, wup_ref[:, lo:lo + FF_CHUNK])
        v = _dot(h, wup_ref[:, d_ff + lo:d_ff + lo + FF_CHUNK])
        g_prev = jnp.where(has_prev, _shift_rows(g, 1), 0.0)
        g_next = jnp.where(has_next, _shift_rows(g, -1), 0.0)
        cw = cw_ref[:, lo:lo + FF_CHUNK]
        cv = (cb_ref[:, lo:lo + FF_CHUNK] + cw[0:1, :] * g_prev
              + cw[1:2, :] * g + cw[2:3, :] * g_next)
        act_ref[:, lo:lo + FF_CHUNK] = (_silu(cv) * v).astype(BF16)
        yield


def _ffn_down(x, mod_ref, wdn_ref, fg_ref, act_ref, *, final_norm):
    y = x + mod_ref[5:6, :] * _dot(act_ref[...], wdn_ref[...])
    if final_norm:
        ms = jnp.mean(y * y, axis=-1, keepdims=True)
        y = y * lax.rsqrt(ms + EPS) * fg_ref[...]
    return y


def _ffn_kernel(x_ref, mod_ref, g_ref, wup_ref, cw_ref, cb_ref, wdn_ref, fg_ref, o_ref,
                act_ref, *, period, final_norm):
    x = x_ref[...]
    _drain(_ffn_up(_ffn_in(x, mod_ref, g_ref), wup_ref, cw_ref, cb_ref, act_ref,
                   period=period))
    o_ref[...] = _ffn_down(x, mod_ref, wdn_ref, fg_ref, act_ref, final_norm=final_norm)


def _conv_ffn(x, ffn, *, tile, period, final_norm):
    t, d = x.shape
    d_ff = ffn[5][0].shape[-2]
    row = lambda k: (k, 0)
    specs, args = _picked(ffn)
    return pl.pallas_call(
        functools.partial(_ffn_kernel, period=period, final_norm=final_norm),
        grid=(t // tile,),
        in_specs=[pl.BlockSpec((tile, d), row), *specs],
        out_specs=pl.BlockSpec((tile, d), row),
        out_shape=jax.ShapeDtypeStruct((t, d), F32),
        scratch_shapes=[pltpu.VMEM((tile, d_ff), BF16)],
        compiler_params=_params(1),
        name="conv_ffn",
    )(x, *args)


def _halo_maps(tile, halo, t, tile_of_step):
    per_tile = tile // halo
    last = t // halo - 1
    prev = lambda k: (jnp.maximum(tile_of_step(k) * per_tile - 1, 0), 0)
    nxt = lambda k: (jnp.minimum((tile_of_step(k) + 1) * per_tile, last), 0)
    return prev, nxt


def _with_halo(xp_ref, x_ref, xn_ref):
    return jnp.concatenate([xp_ref[...], x_ref[...], xn_ref[...]], axis=0)


def _zero_outside(v, halo, tile_idx, n_tiles):
    n = v.shape[0]
    head = jnp.where(tile_idx > 0, v[:halo], 0.0)
    tail = jnp.where(tile_idx < n_tiles - 1, v[n - halo:], 0.0)
    return jnp.concatenate([head, v[halo:n - halo], tail], axis=0)


def _shortconv_kernel(xp_ref, x_ref, xn_ref, mod_ref, g_ref, win_ref, cw_ref, cb_ref,
                      wout_ref, o_ref, *, shift, halo):
    k = pl.program_id(0)
    x = x_ref[...]
    tt, d = x.shape
    xe = _with_halo(xp_ref, x_ref, xn_ref)
    he = _norm_mod(xe, g_ref[...], mod_ref[0:1, :], mod_ref[1:2, :]).astype(BF16)
    cv = _dot(he, win_ref[:, d:3 * d])
    z = _zero_outside(cv[:, :d] * cv[:, d:], halo, k, pl.num_programs(0))
    gate_b = _dot(he[halo:halo + tt], win_ref[:, 0:d])
    conv = (cb_ref[...]
            + cw_ref[0:1, :] * _shift_rows(z, shift)[halo:halo + tt]
            + cw_ref[1:2, :] * z[halo:halo + tt]
            + cw_ref[2:3, :] * _shift_rows(z, -shift)[halo:halo + tt])
    y = _dot((gate_b * conv).astype(BF16), wout_ref[...])
    o_ref[...] = x + mod_ref[2:3, :] * y


def _shortconv(x, mixer, *, tile, shift):
    t, d = x.shape
    halo = max(shift, SUBLANES)
    prev, nxt = _halo_maps(tile, halo, t, lambda k: k)
    row = lambda k: (k, 0)
    specs, args = _picked(mixer)
    return pl.pallas_call(
        functools.partial(_shortconv_kernel, shift=shift, halo=halo),
        grid=(t // tile,),
        in_specs=[
            pl.BlockSpec((halo, d), prev),
            pl.BlockSpec((tile, d), row),
            pl.BlockSpec((halo, d), nxt),
            *specs,
        ],
        out_specs=pl.BlockSpec((tile, d), row),
        out_shape=jax.ShapeDtypeStruct((t, d), F32),
        compiler_params=_params(1),
        name="shortconv",
    )(x, x, x, *args)


def _rglru_coeffs(u, wr_ref, br_ref, wi_ref, bi_ref, lam_ref, a_ref, b_ref):
    hb = wr_ref.shape[1]
    for hd in range(LRU_HEADS):
        cols = slice(hd * hb, (hd + 1) * hb)
        uh = u[:, cols]
        ub = uh.astype(BF16)
        r2 = jnp.tanh(0.5 * (_dot(ub, wr_ref[hd]) + br_ref[:, cols])) + 1.0
        i2 = jnp.tanh(0.5 * (_dot(ub, wi_ref[hd]) + bi_ref[:, cols])) + 1.0
        lam = lam_ref[:, cols]
        log_sig = jnp.minimum(lam, 0.0) - jnp.log1p(jnp.exp(-jnp.abs(lam)))
        half_c = (0.5 * LRU_C) * log_sig
        a = jnp.exp2(r2 * (half_c * LOG2_E))
        one_minus_a2 = jnp.tanh(r2 * (-half_c)) * (a * a + 1.0)
        root = jnp.where(one_minus_a2 > 0.0, one_minus_a2 * lax.rsqrt(one_minus_a2), 0.0)
        a_ref[:, cols] = a
        b_ref[:, cols] = root * (i2 * (0.5 * uh))
        yield


def _gelu_tanh(x):
    c = 0.7978845608028654
    t = jnp.tanh(x * ((x * x) * (0.044715 * c) + c))
    hx = 0.5 * x
    return hx * t + hx


def _silu(x):
    hx = 0.5 * x
    return hx * jnp.tanh(hx) + hx


def _slab_scratch(tile, w):
    return pltpu.VMEM((w // LANES, tile, LANES), F32)


def _lockstep_blocks(tile):
    seg = tile // SUBLANES
    return [(s, q) for s in range(SUBLANES) for q in range(seg // SUBLANES)], seg


def _to_lockstep(v, slab_ref):
    tt, w = v.shape
    blocks, seg = _lockstep_blocks(tt)
    for j in range(w // LANES):
        for s, q in blocks:
            r0 = s * seg + q * SUBLANES
            slab_ref[j, pl.ds(q * SUBLANES * SUBLANES + s, SUBLANES, stride=SUBLANES), :] = (
                v[r0:r0 + SUBLANES, j * LANES:(j + 1) * LANES])
    return jnp.concatenate([slab_ref[j] for j in range(w // LANES)], axis=1)


def _from_lockstep(slab_ref):
    n_lane, tt, _ = slab_ref.shape
    blocks, _ = _lockstep_blocks(tt)
    return jnp.concatenate(
        [jnp.concatenate(
            [slab_ref[j, pl.ds(q * SUBLANES * SUBLANES + s, SUBLANES, stride=SUBLANES), :]
             for s, q in blocks], axis=0) for j in range(n_lane)], axis=1)


def _step_rows(i):
    if isinstance(i, int):
        return pl.ds(i * SUBLANES, SUBLANES)
    return pl.ds(pl.multiple_of(i * SUBLANES, SUBLANES), SUBLANES)


def _scan_lockstep(a_ref, b_ref, carry_ref, reverse, emit, *, straight_line=False):
    tt, w = a_ref.shape
    seg = tt // SUBLANES

    def run(body, init):
        if not straight_line:
            return lax.fori_loop(0, seg, body, init, unroll=2)
        for i in range(seg):
            init = body(i, init)
        return init

    def rows(ref, i):
        return ref[_step_rows(i), :]

    def step_of(i):
        return seg - 1 - i if reverse else i

    def totals(i, carry):
        prod, acc = carry
        av = rows(a_ref, step_of(i))
        return av * prod, av * acc + rows(b_ref, step_of(i))

    init = (jnp.ones((SUBLANES, w), F32), jnp.zeros((SUBLANES, w), F32))
    prod, acc = run(totals, init)

    state = carry_ref[0:1, :]
    starts = [None] * SUBLANES
    for s in (range(SUBLANES - 1, -1, -1) if reverse else range(SUBLANES)):
        starts[s] = state
        state = prod[s:s + 1, :] * state + acc[s:s + 1, :]
    carry_ref[...] = jnp.broadcast_to(state, carry_ref.shape)

    def apply(i, h):
        t = step_of(i)
        h = rows(a_ref, t) * h + rows(b_ref, t)
        emit(t, h)
        return h

    run(apply, jnp.concatenate(starts, axis=0))


def _rglru_bwd_kernel(xp_ref, x_ref, xn_ref, mod_ref, g_ref, wx_ref, cw_ref, cb_ref,
                      wr_ref, br_ref, wi_ref, bi_ref, lam_ref, h0_ref,
                      u_ref, hb_ref, cout_ref, a_ref, b_ref, slab_ref, carry_ref):
    k = pl.program_id(0)
    n_tiles = pl.num_programs(0)
    tt = x_ref.shape[0]

    @pl.when(k == 0)
    def _():
        carry_ref[...] = h0_ref[...]

    xe = _with_halo(xp_ref, x_ref, xn_ref)
    he = _norm_mod(xe, g_ref[...], mod_ref[0:1, :], mod_ref[1:2, :]).astype(BF16)
    pre = _zero_outside(_dot(he, wx_ref[...]), SEQ_HALO, n_tiles - 1 - k, n_tiles)
    lo, hi = SEQ_HALO, SEQ_HALO + tt
    u = (cb_ref[...]
         + cw_ref[0:1, :] * _shift_rows(pre, 2)[lo:hi]
         + cw_ref[1:2, :] * _shift_rows(pre, 1)[lo:hi]
         + cw_ref[2:3, :] * pre[lo:hi]
         + cw_ref[3:4, :] * _shift_rows(pre, -1)[lo:hi])
    u = _to_lockstep(u, slab_ref)
    u_ref[...] = u
    _drain(_rglru_coeffs(u, wr_ref, br_ref, wi_ref, bi_ref, lam_ref, a_ref, b_ref))

    def emit(i, h):
        hb_ref[_step_rows(i), :] = h

    _scan_lockstep(a_ref, b_ref, carry_ref, True, emit)
    cout_ref[...] = carry_ref[...]


def _rglru_fwd_pre(x, u_ref, mod_ref, g_ref, wg_ref, wr_ref, br_ref, wi_ref, bi_ref, lam_ref,
                   a_ref, b_ref):
    h = _norm_mod(x, g_ref[...], mod_ref[0:1, :], mod_ref[1:2, :]).astype(BF16)
    yield
    w = wg_ref.shape[1]
    blk = w // LRU_HEADS
    parts = []
    for c in range(LRU_HEADS):
        parts.append(_gelu_tanh(_dot(h, wg_ref[:, c * blk:(c + 1) * blk])))
        yield
    yield from _rglru_coeffs(u_ref[...], wr_ref, br_ref, wi_ref, bi_ref, lam_ref, a_ref, b_ref)
    return jnp.concatenate(parts, axis=-1)


def _rglru_fwd_scan(hb_ref, a_ref, b_ref, slab_ref, carry_ref, *, straight_line=False):
    def emit(i, hf):
        both = hf + hb_ref[_step_rows(i), :]
        for j in range(slab_ref.shape[0]):
            slab_ref[j, _step_rows(i), :] = both[:, j * LANES:(j + 1) * LANES]

    _scan_lockstep(a_ref, b_ref, carry_ref, False, emit, straight_line=straight_line)


def _rglru_fwd_post(x, gate, mod_ref, wout_ref, slab_ref):
    mixed = (gate * _from_lockstep(slab_ref)).astype(BF16)
    return x + mod_ref[2:3, :] * _dot(mixed, wout_ref[...])


def _rglru_fwd_kernel(x_ref, u_ref, hb_ref, mod_ref, g_ref, wg_ref,
                      wr_ref, br_ref, wi_ref, bi_ref, lam_ref, wout_ref, h0_ref,
                      o_ref, cout_ref, a_ref, b_ref, slab_ref, carry_ref):
    @pl.when(pl.program_id(0) == 0)
    def _():
        carry_ref[...] = h0_ref[...]

    x = x_ref[...]
    gate, = _drain(_rglru_fwd_pre(x, u_ref, mod_ref, g_ref, wg_ref, wr_ref, br_ref, wi_ref,
                                  bi_ref, lam_ref, a_ref, b_ref))
    _rglru_fwd_scan(hb_ref, a_ref, b_ref, slab_ref, carry_ref)
    cout_ref[...] = carry_ref[...]
    o_ref[...] = _rglru_fwd_post(x, gate, mod_ref, wout_ref, slab_ref)


def _rglru_fwd_ffn_kernel(x_ref, u_ref, hb_ref, mod_ref, gmix_ref, wg_ref,
                          wr_ref, br_ref, wi_ref, bi_ref, lam_ref, wout_ref, h0_ref,
                          gffn_ref, wup_ref, cw_ref, cb_ref, wdn_ref, fg_ref,
                          o_ref, a_ref, b_ref, slab_ref, carry_ref, xmid_ref, act_ref, *,
                          period, final_norm):
    @pl.when(pl.program_id(0) == 0)
    def _():
        carry_ref[...] = h0_ref[...]
        xmid_ref[...] = jnp.zeros_like(xmid_ref)

    x_prev, x_cur = xmid_ref[...], x_ref[...]
    _, gate = _drain(
        _ffn_up(_ffn_in(x_prev, mod_ref, gffn_ref), wup_ref, cw_ref, cb_ref, act_ref,
                period=period),
        _rglru_fwd_pre(x_cur, u_ref, mod_ref, gmix_ref, wg_ref, wr_ref, br_ref, wi_ref, bi_ref,
                       lam_ref, a_ref, b_ref),
        head_start=1)
    _rglru_fwd_scan(hb_ref, a_ref, b_ref, slab_ref, carry_ref, straight_line=True)
    o_ref[...] = _ffn_down(x_prev, mod_ref, wdn_ref, fg_ref, act_ref, final_norm=final_norm)
    xmid_ref[...] = _rglru_fwd_post(x_cur, gate, mod_ref, wout_ref, slab_ref)


def _scan_scratch(tile, w):
    return [pltpu.VMEM((tile, w), F32), pltpu.VMEM((tile, w), F32), _slab_scratch(tile, w),
            pltpu.VMEM((SUBLANES, w), F32)]


def _rglru_bwd(x, consts, *, tile):
    t, d = x.shape
    w = consts[2][0].shape[-1]
    n_tiles = t // tile
    rev = lambda k: n_tiles - 1 - k
    prev, nxt = _halo_maps(tile, SEQ_HALO, t, rev)
    row = lambda k: (rev(k), 0)
    specs, args = _picked(consts)
    return pl.pallas_call(
        _rglru_bwd_kernel,
        grid=(n_tiles,),
        in_specs=[
            pl.BlockSpec((SEQ_HALO, d), prev),
            pl.BlockSpec((tile, d), row),
            pl.BlockSpec((SEQ_HALO, d), nxt),
            *specs,
        ],
        out_specs=[
            pl.BlockSpec((tile, w), row),
            pl.BlockSpec((tile, w), row),
            pl.BlockSpec((SUBLANES, w), lambda k: (0, 0)),
        ],
        out_shape=[
            jax.ShapeDtypeStruct((t, w), F32),
            jax.ShapeDtypeStruct((t, w), F32),
            jax.ShapeDtypeStruct((SUBLANES, w), F32),
        ],
        scratch_shapes=_scan_scratch(tile, w),
        compiler_params=_params(1),
        name="rglru_bwd",
    )(x, x, x, *args)


def _rglru_fwd(x, u, hb, consts, *, tile):
    t, d = x.shape
    w = u.shape[1]
    row = lambda k: (k, 0)
    specs, args = _picked(consts)
    return pl.pallas_call(
        _rglru_fwd_kernel,
        grid=(t // tile,),
        in_specs=[
            pl.BlockSpec((tile, d), row),
            pl.BlockSpec((tile, w), row),
            pl.BlockSpec((tile, w), row),
            *specs,
        ],
        out_specs=[
            pl.BlockSpec((tile, d), row),
            pl.BlockSpec((SUBLANES, w), lambda k: (0, 0)),
        ],
        out_shape=[
            jax.ShapeDtypeStruct((t, d), F32),
            jax.ShapeDtypeStruct((SUBLANES, w), F32),
        ],
        scratch_shapes=_scan_scratch(tile, w),
        compiler_params=_params(1),
        name="rglru_fwd",
    )(x, u, hb, *args)


def _rglru_fwd_ffn(x, u, hb, consts, *, tile, period, final_norm):
    t, d = x.shape
    w = u.shape[1]
    d_ff = consts[-2][0].shape[-2]
    n_tiles = t // tile
    cur = lambda k: (jnp.minimum(k, n_tiles - 1), 0)
    done = lambda k: (jnp.maximum(k - 1, 0), 0)
    specs, args = _picked(consts)
    return pl.pallas_call(
        functools.partial(_rglru_fwd_ffn_kernel, period=period, final_norm=final_norm),
        grid=(n_tiles + 1,),
        in_specs=[
            pl.BlockSpec((tile, d), cur),
            pl.BlockSpec((tile, w), cur),
            pl.BlockSpec((tile, w), cur),
            *specs,
        ],
        out_specs=pl.BlockSpec((tile, d), done),
        out_shape=jax.ShapeDtypeStruct((t, d), F32),
        scratch_shapes=[
            *_scan_scratch(tile, w),
            pltpu.VMEM((tile, d), F32),
            pltpu.VMEM((tile, d_ff), BF16),
        ],
        compiler_params=_params(1),
        name="rglru_fwd_ffn",
    )(x, u, hb, *args)


def kernel(x, c, ctx, c_ctx, ada_w, ada_b, norm_mix_g, norm_ffn_g, a_w_x, a_w_gate, a_conv_w, a_conv_b, a_w_r, a_b_r, a_w_i, a_b_i, a_lambda, a_w_out, b_w_in, b_conv_w, b_conv_b, b_w_out, f_w_up, f_conv_w, f_conv_b, f_w_down, final_g):
    bsz, seq, d = x.shape
    ctx_len = ctx.shape[1]
    depth = ada_w.shape[0]
    n_mixers = 2
    assert bsz == 1 and GRID_W % SUBLANES == 0
    assert all(seq % tile == 0 and tile % GRID_W == 0
               for tile in (TOKEN_TILE, RG_TILE, FFN_TILE))
    ctx_tile = ctx_len

    cvecs = jnp.zeros((MOD_ROWS, d), F32).at[0].set(c[0]).at[1].set(c_ctx)
    mods = _adaln_all(cvecs, ada_w, ada_b).reshape(depth, MOD_ROWS, N_MOD, d)
    mods = jnp.pad(mods[:, :2], ((0, 0), (0, 0), (0, MOD_ROWS - N_MOD), (0, 0)))
    lat, cx = 0, 1

    rows = lambda v: v.reshape(*v.shape[:-1], 1, v.shape[-1])
    g_mix, g_ffn, fin_g = rows(norm_mix_g), rows(norm_ffn_g), final_g.reshape(1, d)
    w_x, w_gate, w_out_a = a_w_x.astype(BF16), a_w_gate.astype(BF16), a_w_out.astype(BF16)
    w_r, w_i = a_w_r.astype(BF16), a_w_i.astype(BF16)
    lru_w = a_w_x.shape[2]
    b_r, b_i = (v.reshape(*v.shape[:2], 1, lru_w) for v in (a_b_r, a_b_i))
    lam, conv_b_a = rows(a_lambda), rows(a_conv_b)
    w_in, w_out_b, conv_b_b = b_w_in.astype(BF16), b_w_out.astype(BF16), rows(b_conv_b)
    w_up, w_down, conv_b_f = f_w_up.astype(BF16), f_w_down.astype(BF16), rows(f_conv_b)

    xl, xc = x[0], ctx[0]
    zero_state = jnp.zeros((SUBLANES, lru_w), F32)
    for l in range(depth):
        j = l // n_mixers
        ctx_needed = any(k % n_mixers == 0 for k in range(l + 1, depth))
        final_norm = l == depth - 1
        ffn = [_pick(g_ffn, l), _pick(w_up, l), _pick(f_conv_w, l), _pick(conv_b_f, l),
               _pick(w_down, l), _pick(fin_g)]
        if l % n_mixers == 0:
            gate = lambda dr: [_pick(w_r, j, dr), _pick(b_r, j, dr), _pick(w_i, j, dr),
                               _pick(b_i, j, dr), _pick(lam, j, dr)]
            bwd = lambda s, h0: [_pick(mods, l, s), _pick(g_mix, l), _pick(w_x, j),
                                 _pick(a_conv_w, j), _pick(conv_b_a, j), *gate(1), _pick(h0)]
            fwd = lambda s, h0: [_pick(mods, l, s), _pick(g_mix, l), _pick(w_gate, j),
                                 *gate(0), _pick(w_out_a, j), _pick(h0)]
            u_c, hb_c, end_b = _rglru_bwd(xc, bwd(cx, zero_state), tile=ctx_tile)
            xc_new, end_f = _rglru_fwd(xc, u_c, hb_c, fwd(cx, zero_state), tile=ctx_tile)
            u_l, hb_l, _ = _rglru_bwd(xl, bwd(lat, end_b), tile=RG_TILE)
            xl = _rglru_fwd_ffn(xl, u_l, hb_l, fwd(lat, end_f) + ffn, tile=RG_TILE,
                                period=GRID_W, final_norm=final_norm)
            if ctx_needed:
                xc = xc_new
        else:
            mixer = lambda s: [_pick(mods, l, s), _pick(g_mix, l), _pick(w_in, j),
                               _pick(b_conv_w, j), _pick(conv_b_b, j), _pick(w_out_b, j)]
            xl = _shortconv(xl, mixer(lat), tile=TOKEN_TILE, shift=GRID_W)
            xl = _conv_ffn(xl, [_pick(mods, l, lat)] + ffn, tile=FFN_TILE, period=GRID_W,
                           final_norm=final_norm)
            if ctx_needed:
                xc = _shortconv(xc, mixer(cx), tile=ctx_tile, shift=1)
        if ctx_needed:
            xc = _conv_ffn(xc, [_pick(mods, l, cx)] + ffn, tile=ctx_tile, period=ctx_len,
                           final_norm=False)
    return xl[None]
```

```python
import functools

import jax
import jax.numpy as jnp
from jax import lax
from jax.experimental import pallas as pl
from jax.experimental.pallas import tpu as pltpu

F32 = jnp.float32
BF16 = jnp.bfloat16

EPS = 1e-6
LRU_C = 8.0
LRU_HEADS = 4
GRID_W = 64
N_MOD = 6
MOD_ROWS = 8
SUBLANES = 8
LANES = 128
BF16_ROWS = 16
SEQ_HALO = 8
TOKEN_TILE = 1024
RG_TILE = 512
FFN_TILE = 1024
LOG2_E = 1.4426950408889634
FF_CHUNK = 256
ADA_COLS = 1536
VMEM_LIMIT = 56 * 1024 * 1024


def _dot(a, b):
    return jnp.dot(a, b, preferred_element_type=F32)


def _norm_mod(x, g, shift, scale):
    ms = jnp.mean(x * x, axis=-1, keepdims=True)
    return (x * lax.rsqrt(ms + EPS)) * (g * (1.0 + scale)) + shift


def _shift_rows(x, s):
    n = x.shape[0]
    return pltpu.roll(x, s % n, 0)


def _pick(arr, *idx):
    return arr, idx


def _picked(picks):
    specs = []
    for arr, idx in picks:
        shape = arr.shape[len(idx):]
        at = idx + (0,) * len(shape)
        specs.append(pl.BlockSpec((None,) * len(idx) + shape, lambda *_, at=at: at,
                                  pipeline_mode=pl.Buffered(1)))
    return specs, [arr for arr, _ in picks]


def _cast_jobs(casts, n_steps):
    in_specs, out_specs, out_shapes, arrays = [], [], [], []
    for arr, idx in casts:
        rows, cols = arr.shape[len(idx):]
        blk = next(b for b in range(BF16_ROWS, rows + 1, BF16_ROWS)
                   if rows % b == 0 and rows // b <= n_steps)
        last = rows // blk - 1
        in_specs.append(pl.BlockSpec(
            (None,) * len(idx) + (blk, cols),
            lambda k, idx=idx, last=last: idx + (jnp.minimum(k, last), 0)))
        out_specs.append(pl.BlockSpec((blk, cols), lambda k, last=last: (jnp.minimum(k, last), 0)))
        out_shapes.append(jax.ShapeDtypeStruct((rows, cols), BF16))
        arrays.append(arr)
    return in_specs, out_specs, out_shapes, arrays


def _riding_casts(body, n_in, n_out, n_cast):
    if not n_cast:
        return body

    def kern(*refs):
        ins, rest = refs[:n_in], refs[n_in:]
        srcs, rest = rest[:n_cast], rest[n_cast:]
        outs, rest = rest[:n_out], rest[n_out:]
        dsts, scratch = rest[:n_cast], rest[n_cast:]
        body(*ins, *outs, *scratch)
        for src, dst in zip(srcs, dsts):
            dst[...] = src[...].astype(BF16)

    return kern


def _params(n_axes):
    return pltpu.CompilerParams(
        dimension_semantics=("arbitrary",) * n_axes, vmem_limit_bytes=VMEM_LIMIT)


def _ada_kernel(c_ref, w_ref, b_ref, o_ref):
    cv = c_ref[...]
    s = jax.nn.silu(cv).astype(BF16)
    o_ref[0] = _dot(s, w_ref[0].astype(BF16)) + b_ref[0]


def _adaln_all(cvecs, ada_w, ada_b):
    depth, d, n = ada_w.shape
    return pl.pallas_call(
        _ada_kernel,
        grid=(depth, n // ADA_COLS),
        in_specs=[
            pl.BlockSpec((MOD_ROWS, d), lambda l, c: (0, 0)),
            pl.BlockSpec((1, d, ADA_COLS), lambda l, c: (l, 0, c)),
            pl.BlockSpec((1, 1, ADA_COLS), lambda l, c: (l, 0, c)),
        ],
        out_specs=pl.BlockSpec((1, MOD_ROWS, ADA_COLS), lambda l, c: (l, 0, c)),
        out_shape=jax.ShapeDtypeStruct((depth, MOD_ROWS, n), F32),
        compiler_params=_params(2),
        name="adaln",
    )(cvecs, ada_w, ada_b.reshape(depth, 1, n))


def _drain(*stages, head_start=0):
    results = [None] * len(stages)
    live = dict(enumerate(stages))
    for _ in range(head_start):
        next(stages[0])
    while live:
        for idx in list(live):
            try:
                next(live[idx])
            except StopIteration as stop:
                results[idx] = stop.value
                del live[idx]
    return results


def _ffn_in(x, mod_ref, g_ref):
    return _norm_mod(x, g_ref[...], mod_ref[3:4, :], mod_ref[4:5, :]).astype(BF16)


def _ffn_up(h, wup_ref, cw_ref, cb_ref, act_ref, *, period):
    tt = h.shape[0]
    d_ff = act_ref.shape[1]
    pos = lax.broadcasted_iota(jnp.int32, (tt, FF_CHUNK), 0) % period
    has_prev = pos != 0
    has_next = pos != period - 1
    for c in range(d_ff // FF_CHUNK):
        lo = c * FF_CHUNK
        g = _dot(h, wup_ref[:, lo:lo + FF_CHUNK])
        v = _dot(h, wup_ref[:, d_ff + lo:d_ff + lo + FF_CHUNK])
        g_prev = jnp.where(has_prev, _shift_rows(g, 1), 0.0)
        g_next = jnp.where(has_next, _shift_rows(g, -1), 0.0)
        cw = cw_ref[:, lo:lo + FF_CHUNK]
        cv = (cb_ref[:, lo:lo + FF_CHUNK] + cw[0:1, :] * g_prev
              + cw[1:2, :] * g + cw[2:3, :] * g_next)
        act_ref[:, lo:lo + FF_CHUNK] = (_silu(cv) * v).astype(BF16)
        yield


def _ffn_down(x, mod_ref, wdn_ref, fg_ref, act_ref, *, final_norm):
    y = x + mod_ref[5:6, :] * _dot(act_ref[...], wdn_ref[...])
    if final_norm:
        ms = jnp.mean(y * y, axis=-1, keepdims=True)
        y = y * lax.rsqrt(ms + EPS) * fg_ref[...]
    return y


def _ffn_kernel(x_ref, mod_ref, g_ref, wup_ref, cw_ref, cb_ref, wdn_ref, fg_ref, o_ref,
                act_ref, *, period, final_norm):
    x = x_ref[...]
    _drain(_ffn_up(_ffn_in(x, mod_ref, g_ref), wup_ref, cw_ref, cb_ref, act_ref,
                   period=period))
    o_ref[...] = _ffn_down(x, mod_ref, wdn_ref, fg_ref, act_ref, final_norm=final_norm)


def _conv_ffn(x, ffn, *, tile, period, final_norm, casts=()):
    t, d = x.shape
    d_ff = ffn[5][0].shape[-2]
    row = lambda k: (k, 0)
    specs, args = _picked(ffn)
    c_in, c_out, c_shapes, c_args = _cast_jobs(casts, t // tile)
    body = functools.partial(_ffn_kernel, period=period, final_norm=final_norm)
    return pl.pallas_call(
        _riding_casts(body, 1 + len(args), 1, len(casts)),
        grid=(t // tile,),
        in_specs=[pl.BlockSpec((tile, d), row), *specs, *c_in],
        out_specs=[pl.BlockSpec((tile, d), row), *c_out],
        out_shape=[jax.ShapeDtypeStruct((t, d), F32), *c_shapes],
        scratch_shapes=[pltpu.VMEM((tile, d_ff), BF16)],
        compiler_params=_params(1),
        name="conv_ffn",
    )(x, *args, *c_args)


def _halo_maps(tile, halo, t, tile_of_step):
    per_tile = tile // halo
    last = t // halo - 1
    prev = lambda k: (jnp.maximum(tile_of_step(k) * per_tile - 1, 0), 0)
    nxt = lambda k: (jnp.minimum((tile_of_step(k) + 1) * per_tile, last), 0)
    return prev, nxt


def _with_halo(xp_ref, x_ref, xn_ref):
    return jnp.concatenate([xp_ref[...], x_ref[...], xn_ref[...]], axis=0)


def _zero_outside(v, halo, tile_idx, n_tiles):
    n = v.shape[0]
    head = jnp.where(tile_idx > 0, v[:halo], 0.0)
    tail = jnp.where(tile_idx < n_tiles - 1, v[n - halo:], 0.0)
    return jnp.concatenate([head, v[halo:n - halo], tail], axis=0)


def _shortconv_kernel(xp_ref, x_ref, xn_ref, mod_ref, g_ref, win_ref, cw_ref, cb_ref,
                      wout_ref, o_ref, *, shift, halo):
    k = pl.program_id(0)
    x = x_ref[...]
    tt, d = x.shape
    xe = _with_halo(xp_ref, x_ref, xn_ref)
    he = _norm_mod(xe, g_ref[...], mod_ref[0:1, :], mod_ref[1:2, :]).astype(BF16)
    cv = _dot(he, win_ref[:, d:3 * d])
    z = _zero_outside(cv[:, :d] * cv[:, d:], halo, k, pl.num_programs(0))
    gate_b = _dot(he[halo:halo + tt], win_ref[:, 0:d])
    conv = (cb_ref[...]
            + cw_ref[0:1, :] * _shift_rows(z, shift)[halo:halo + tt]
            + cw_ref[1:2, :] * z[halo:halo + tt]
            + cw_ref[2:3, :] * _shift_rows(z, -shift)[halo:halo + tt])
    y = _dot((gate_b * conv).astype(BF16), wout_ref[...])
    o_ref[...] = x + mod_ref[2:3, :] * y


def _shortconv(x, mixer, *, tile, shift, casts=()):
    t, d = x.shape
    halo = max(shift, SUBLANES)
    prev, nxt = _halo_maps(tile, halo, t, lambda k: k)
    row = lambda k: (k, 0)
    specs, args = _picked(mixer)
    c_in, c_out, c_shapes, c_args = _cast_jobs(casts, t // tile)
    body = functools.partial(_shortconv_kernel, shift=shift, halo=halo)
    return pl.pallas_call(
        _riding_casts(body, 3 + len(args), 1, len(casts)),
        grid=(t // tile,),
        in_specs=[
            pl.BlockSpec((halo, d), prev),
            pl.BlockSpec((tile, d), row),
            pl.BlockSpec((halo, d), nxt),
            *specs,
            *c_in,
        ],
        out_specs=[pl.BlockSpec((tile, d), row), *c_out],
        out_shape=[jax.ShapeDtypeStruct((t, d), F32), *c_shapes],
        compiler_params=_params(1),
        name="shortconv",
    )(x, x, x, *args, *c_args)


def _rglru_coeffs(u, wr_ref, br_ref, wi_ref, bi_ref, lam_ref, a_ref, b_ref):
    hb = wr_ref.shape[1]
    for hd in range(LRU_HEADS):
        cols = slice(hd * hb, (hd + 1) * hb)
        uh = u[:, cols]
        ub = uh.astype(BF16)
        r2 = jnp.tanh(0.5 * (_dot(ub, wr_ref[hd]) + br_ref[:, cols])) + 1.0
        i2 = jnp.tanh(0.5 * (_dot(ub, wi_ref[hd]) + bi_ref[:, cols])) + 1.0
        lam = lam_ref[:, cols]
        log_sig = jnp.minimum(lam, 0.0) - jnp.log1p(jnp.exp(-jnp.abs(lam)))
        half_c = (0.5 * LRU_C) * log_sig
        a = jnp.exp2(r2 * (half_c * LOG2_E))
        one_minus_a2 = jnp.tanh(r2 * (-half_c)) * (a * a + 1.0)
        root = jnp.where(one_minus_a2 > 0.0, one_minus_a2 * lax.rsqrt(one_minus_a2), 0.0)
        a_ref[:, cols] = a
        b_ref[:, cols] = root * (i2 * (0.5 * uh))
        yield


def _gelu_tanh(x):
    c = 0.7978845608028654
    t = jnp.tanh(x * ((x * x) * (0.044715 * c) + c))
    hx = 0.5 * x
    return hx * t + hx


def _silu(x):
    hx = 0.5 * x
    return hx * jnp.tanh(hx) + hx


def _slab_scratch(tile, w):
    return pltpu.VMEM((w // LANES, tile, LANES), F32)


def _lockstep_blocks(tile):
    seg = tile // SUBLANES
    return [(s, q) for s in range(SUBLANES) for q in range(seg // SUBLANES)], seg


def _to_lockstep(v, slab_ref):
    tt, w = v.shape
    blocks, seg = _lockstep_blocks(tt)
    for j in range(w // LANES):
        for s, q in blocks:
            r0 = s * seg + q * SUBLANES
            slab_ref[j, pl.ds(q * SUBLANES * SUBLANES + s, SUBLANES, stride=SUBLANES), :] = (
                v[r0:r0 + SUBLANES, j * LANES:(j + 1) * LANES])
    return jnp.concatenate([slab_ref[j] for j in range(w // LANES)], axis=1)


def _from_lockstep(slab_ref):
    n_lane, tt, _ = slab_ref.shape
    blocks, _ = _lockstep_blocks(tt)
    return jnp.concatenate(
        [jnp.concatenate(
            [slab_ref[j, pl.ds(q * SUBLANES * SUBLANES + s, SUBLANES, stride=SUBLANES), :]
             for s, q in blocks], axis=0) for j in range(n_lane)], axis=1)


def _step_rows(i):
    if isinstance(i, int):
        return pl.ds(i * SUBLANES, SUBLANES)
    return pl.ds(pl.multiple_of(i * SUBLANES, SUBLANES), SUBLANES)


def _scan_lockstep(a_ref, b_ref, carry_ref, reverse, emit, *, straight_line=False):
    tt, w = a_ref.shape
    seg = tt // SUBLANES

    def run(body, init):
        if not straight_line:
            return lax.fori_loop(0, seg, body, init, unroll=2)
        for i in range(seg):
            init = body(i, init)
        return init

    def rows(ref, i):
        return ref[_step_rows(i), :]

    def step_of(i):
        return seg - 1 - i if reverse else i

    def totals(i, carry):
        prod, acc = carry
        av = rows(a_ref, step_of(i))
        return av * prod, av * acc + rows(b_ref, step_of(i))

    init = (jnp.ones((SUBLANES, w), F32), jnp.zeros((SUBLANES, w), F32))
    prod, acc = run(totals, init)

    state = carry_ref[0:1, :]
    starts = [None] * SUBLANES
    for s in (range(SUBLANES - 1, -1, -1) if reverse else range(SUBLANES)):
        starts[s] = state
        state = prod[s:s + 1, :] * state + acc[s:s + 1, :]
    carry_ref[...] = jnp.broadcast_to(state, carry_ref.shape)

    def apply(i, h):
        t = step_of(i)
        h = rows(a_ref, t) * h + rows(b_ref, t)
        emit(t, h)
        return h

    run(apply, jnp.concatenate(starts, axis=0))


def _rglru_bwd_kernel(xp_ref, x_ref, xn_ref, mod_ref, g_ref, wx_ref, cw_ref, cb_ref,
                      wr_ref, br_ref, wi_ref, bi_ref, lam_ref, h0_ref,
                      u_ref, hb_ref, cout_ref, a_ref, b_ref, slab_ref, carry_ref):
    k = pl.program_id(0)
    n_tiles = pl.num_programs(0)
    tt = x_ref.shape[0]

    @pl.when(k == 0)
    def _():
        carry_ref[...] = h0_ref[...]

    xe = _with_halo(xp_ref, x_ref, xn_ref)
    he = _norm_mod(xe, g_ref[...], mod_ref[0:1, :], mod_ref[1:2, :]).astype(BF16)
    pre = _zero_outside(_dot(he, wx_ref[...]), SEQ_HALO, n_tiles - 1 - k, n_tiles)
    lo, hi = SEQ_HALO, SEQ_HALO + tt
    u = (cb_ref[...]
         + cw_ref[0:1, :] * _shift_rows(pre, 2)[lo:hi]
         + cw_ref[1:2, :] * _shift_rows(pre, 1)[lo:hi]
         + cw_ref[2:3, :] * pre[lo:hi]
         + cw_ref[3:4, :] * _shift_rows(pre, -1)[lo:hi])
    u = _to_lockstep(u, slab_ref)
    u_ref[...] = u
    _drain(_rglru_coeffs(u, wr_ref, br_ref, wi_ref, bi_ref, lam_ref, a_ref, b_ref))

    def emit(i, h):
        hb_ref[_step_rows(i), :] = h

    _scan_lockstep(a_ref, b_ref, carry_ref, True, emit)
    cout_ref[...] = carry_ref[...]


def _rglru_fwd_pre(x, u_ref, mod_ref, g_ref, wg_ref, wr_ref, br_ref, wi_ref, bi_ref, lam_ref,
                   a_ref, b_ref):
    h = _norm_mod(x, g_ref[...], mod_ref[0:1, :], mod_ref[1:2, :]).astype(BF16)
    yield
    w = wg_ref.shape[1]
    blk = w // LRU_HEADS
    parts = []
    for c in range(LRU_HEADS):
        parts.append(_gelu_tanh(_dot(h, wg_ref[:, c * blk:(c + 1) * blk])))
        yield
    yield from _rglru_coeffs(u_ref[...], wr_ref, br_ref, wi_ref, bi_ref, lam_ref, a_ref, b_ref)
    return jnp.concatenate(parts, axis=-1)


def _rglru_fwd_scan(hb_ref, a_ref, b_ref, slab_ref, carry_ref, *, straight_line=False):
    def emit(i, hf):
        both = hf + hb_ref[_step_rows(i), :]
        for j in range(slab_ref.shape[0]):
            slab_ref[j, _step_rows(i), :] = both[:, j * LANES:(j + 1) * LANES]

    _scan_lockstep(a_ref, b_ref, carry_ref, False, emit, straight_line=straight_line)


def _rglru_fwd_post(x, gate, mod_ref, wout_ref, slab_ref):
    mixed = (gate * _from_lockstep(slab_ref)).astype(BF16)
    return x + mod_ref[2:3, :] * _dot(mixed, wout_ref[...])


def _rglru_fwd_kernel(x_ref, u_ref, hb_ref, mod_ref, g_ref, wg_ref,
                      wr_ref, br_ref, wi_ref, bi_ref, lam_ref, wout_ref, h0_ref,
                      o_ref, cout_ref, a_ref, b_ref, slab_ref, carry_ref):
    @pl.when(pl.program_id(0) == 0)
    def _():
        carry_ref[...] = h0_ref[...]

    x = x_ref[...]
    gate, = _drain(_rglru_fwd_pre(x, u_ref, mod_ref, g_ref, wg_ref, wr_ref, br_ref, wi_ref,
                                  bi_ref, lam_ref, a_ref, b_ref))
    _rglru_fwd_scan(hb_ref, a_ref, b_ref, slab_ref, carry_ref)
    cout_ref[...] = carry_ref[...]
    o_ref[...] = _rglru_fwd_post(x, gate, mod_ref, wout_ref, slab_ref)


def _rglru_fwd_ffn_kernel(x_ref, u_ref, hb_ref, mod_ref, gmix_ref, wg_ref,
                          wr_ref, br_ref, wi_ref, bi_ref, lam_ref, wout_ref, h0_ref,
                          gffn_ref, wup_ref, cw_ref, cb_ref, wdn_ref, fg_ref,
                          o_ref, a_ref, b_ref, slab_ref, carry_ref, xmid_ref, act_ref, *,
                          period, final_norm):
    @pl.when(pl.program_id(0) == 0)
    def _():
        carry_ref[...] = h0_ref[...]
        xmid_ref[...] = jnp.zeros_like(xmid_ref)

    x_prev, x_cur = xmid_ref[...], x_ref[...]
    _, gate = _drain(
        _ffn_up(_ffn_in(x_prev, mod_ref, gffn_ref), wup_ref, cw_ref, cb_ref, act_ref,
                period=period),
        _rglru_fwd_pre(x_cur, u_ref, mod_ref, gmix_ref, wg_ref, wr_ref, br_ref, wi_ref, bi_ref,
                       lam_ref, a_ref, b_ref),
        head_start=1)
    _rglru_fwd_scan(hb_ref, a_ref, b_ref, slab_ref, carry_ref, straight_line=True)
    o_ref[...] = _ffn_down(x_prev, mod_ref, wdn_ref, fg_ref, act_ref, final_norm=final_norm)
    xmid_ref[...] = _rglru_fwd_post(x_cur, gate, mod_ref, wout_ref, slab_ref)


def _scan_scratch(tile, w):
    return [pltpu.VMEM((tile, w), F32), pltpu.VMEM((tile, w), F32), _slab_scratch(tile, w),
            pltpu.VMEM((SUBLANES, w), F32)]


def _rglru_bwd(x, consts, *, tile):
    t, d = x.shape
    w = consts[2][0].shape[-1]
    n_tiles = t // tile
    rev = lambda k: n_tiles - 1 - k
    prev, nxt = _halo_maps(tile, SEQ_HALO, t, rev)
    row = lambda k: (rev(k), 0)
    specs, args = _picked(consts)
    return pl.pallas_call(
        _rglru_bwd_kernel,
        grid=(n_tiles,),
        in_specs=[
            pl.BlockSpec((SEQ_HALO, d), prev),
            pl.BlockSpec((tile, d), row),
            pl.BlockSpec((SEQ_HALO, d), nxt),
            *specs,
        ],
        out_specs=[
            pl.BlockSpec((tile, w), row),
            pl.BlockSpec((tile, w), row),
            pl.BlockSpec((SUBLANES, w), lambda k: (0, 0)),
        ],
        out_shape=[
            jax.ShapeDtypeStruct((t, w), F32),
            jax.ShapeDtypeStruct((t, w), F32),
            jax.ShapeDtypeStruct((SUBLANES, w), F32),
        ],
        scratch_shapes=_scan_scratch(tile, w),
        compiler_params=_params(1),
        name="rglru_bwd",
    )(x, x, x, *args)


def _rglru_fwd(x, u, hb, consts, *, tile):
    t, d = x.shape
    w = u.shape[1]
    row = lambda k: (k, 0)
    specs, args = _picked(consts)
    return pl.pallas_call(
        _rglru_fwd_kernel,
        grid=(t // tile,),
        in_specs=[
            pl.BlockSpec((tile, d), row),
            pl.BlockSpec((tile, w), row),
            pl.BlockSpec((tile, w), row),
            *specs,
        ],
        out_specs=[
            pl.BlockSpec((tile, d), row),
            pl.BlockSpec((SUBLANES, w), lambda k: (0, 0)),
        ],
        out_shape=[
            jax.ShapeDtypeStruct((t, d), F32),
            jax.ShapeDtypeStruct((SUBLANES, w), F32),
        ],
        scratch_shapes=_scan_scratch(tile, w),
        compiler_params=_params(1),
        name="rglru_fwd",
    )(x, u, hb, *args)


def _rglru_fwd_ffn(x, u, hb, consts, *, tile, period, final_norm, casts=()):
    t, d = x.shape
    w = u.shape[1]
    d_ff = consts[-2][0].shape[-2]
    n_tiles = t // tile
    cur = lambda k: (jnp.minimum(k, n_tiles - 1), 0)
    done = lambda k: (jnp.maximum(k - 1, 0), 0)
    specs, args = _picked(consts)
    c_in, c_out, c_shapes, c_args = _cast_jobs(casts, n_tiles + 1)
    body = functools.partial(_rglru_fwd_ffn_kernel, period=period, final_norm=final_norm)
    return pl.pallas_call(
        _riding_casts(body, 3 + len(args), 1, len(casts)),
        grid=(n_tiles + 1,),
        in_specs=[
            pl.BlockSpec((tile, d), cur),
            pl.BlockSpec((tile, w), cur),
            pl.BlockSpec((tile, w), cur),
            *specs,
            *c_in,
        ],
        out_specs=[pl.BlockSpec((tile, d), done), *c_out],
        out_shape=[jax.ShapeDtypeStruct((t, d), F32), *c_shapes],
        scratch_shapes=[
            *_scan_scratch(tile, w),
            pltpu.VMEM((tile, d), F32),
            pltpu.VMEM((tile, d_ff), BF16),
        ],
        compiler_params=_params(1),
        name="rglru_fwd_ffn",
    )(x, u, hb, *args, *c_args)


def kernel(x, c, ctx, c_ctx, ada_w, ada_b, norm_mix_g, norm_ffn_g, a_w_x, a_w_gate, a_conv_w, a_conv_b, a_w_r, a_b_r, a_w_i, a_b_i, a_lambda, a_w_out, b_w_in, b_conv_w, b_conv_b, b_w_out, f_w_up, f_conv_w, f_conv_b, f_w_down, final_g):
    bsz, seq, d = x.shape
    ctx_len = ctx.shape[1]
    depth = ada_w.shape[0]
    n_mixers = 2
    assert bsz == 1 and GRID_W % SUBLANES == 0
    assert all(seq % tile == 0 and tile % GRID_W == 0
               for tile in (TOKEN_TILE, RG_TILE, FFN_TILE))
    ctx_tile = ctx_len

    cvecs = jnp.zeros((MOD_ROWS, d), F32).at[0].set(c[0]).at[1].set(c_ctx)
    mods = _adaln_all(cvecs, ada_w, ada_b).reshape(depth, MOD_ROWS, N_MOD, d)
    mods = jnp.pad(mods[:, :2], ((0, 0), (0, 0), (0, MOD_ROWS - N_MOD), (0, 0)))
    lat, cx = 0, 1

    rows = lambda v: v.reshape(*v.shape[:-1], 1, v.shape[-1])
    g_mix, g_ffn, fin_g = rows(norm_mix_g), rows(norm_ffn_g), final_g.reshape(1, d)
    lru_w = a_w_x.shape[2]
    b_r, b_i = (v.reshape(*v.shape[:2], 1, lru_w) for v in (a_b_r, a_b_i))
    lam, conv_b_a, conv_b_b, conv_b_f = (rows(v) for v in (a_lambda, a_conv_b, b_conv_b,
                                                            f_conv_b))
    gate_shape = a_w_r.shape[1:]
    flat = lambda v: v.reshape(v.shape[0], -1, v.shape[-1])
    rg_src = (a_w_x, a_w_gate, a_w_out, flat(a_w_r), flat(a_w_i))
    sc_src = (b_w_in, b_w_out)
    ffn_src = (f_w_up, f_w_down)
    picks_of = lambda srcs, i: [_pick(v, i) for v in srcs]

    rg_w = [v[0].astype(BF16) for v in rg_src]
    ffn_w = [v[0].astype(BF16) for v in ffn_src]
    sc_w = nxt_w = []

    xl, xc = x[0], ctx[0]
    zero_state = jnp.zeros((SUBLANES, lru_w), F32)
    for l in range(depth):
        j, nxt = l // n_mixers, l + 1
        ctx_needed = any(k % n_mixers == 0 for k in range(nxt, depth))
        final_norm = nxt == depth
        ffn = lambda s: [_pick(mods, l, s), _pick(g_ffn, l), _pick(ffn_w[0]), _pick(f_conv_w, l),
                         _pick(conv_b_f, l), _pick(ffn_w[1]), _pick(fin_g)]
        if l % n_mixers == 0:
            w_x, w_gate, w_out = rg_w[:3]
            w_r, w_i = (v.reshape(gate_shape) for v in rg_w[3:])
            gate = lambda dr: [_pick(w_r, dr), _pick(b_r, j, dr), _pick(w_i, dr),
                               _pick(b_i, j, dr), _pick(lam, j, dr)]
            bwd = lambda s, h0: [_pick(mods, l, s), _pick(g_mix, l), _pick(w_x),
                                 _pick(a_conv_w, j), _pick(conv_b_a, j), *gate(1), _pick(h0)]
            fwd = lambda s, h0: [_pick(mods, l, s), _pick(g_mix, l), _pick(w_gate), *gate(0),
                                 _pick(w_out), _pick(h0)]
            u_c, hb_c, end_b = _rglru_bwd(xc, bwd(cx, zero_state), tile=ctx_tile)
            xc_new, end_f = _rglru_fwd(xc, u_c, hb_c, fwd(cx, zero_state), tile=ctx_tile)
            u_l, hb_l, _ = _rglru_bwd(xl, bwd(lat, end_b), tile=RG_TILE)
            casts = picks_of(sc_src, nxt // n_mixers) if nxt < depth else []
            xl, *sc_w = _rglru_fwd_ffn(xl, u_l, hb_l, fwd(lat, end_f) + ffn(lat)[1:],
                                       tile=RG_TILE, period=GRID_W, final_norm=final_norm,
                                       casts=casts)
            if ctx_needed:
                xc = xc_new
        else:
            mixer = lambda s: [_pick(mods, l, s), _pick(g_mix, l), _pick(sc_w[0]),
                               _pick(b_conv_w, j), _pick(conv_b_b, j), _pick(sc_w[1])]
            xl, *ffn_w = _shortconv(xl, mixer(lat), tile=TOKEN_TILE, shift=GRID_W,
                                    casts=picks_of(ffn_src, l))
            casts = (picks_of(rg_src, nxt // n_mixers) + picks_of(ffn_src, nxt)
                     if nxt < depth else [])
            xl, *nxt_w = _conv_ffn(xl, ffn(lat), tile=FFN_TILE, period=GRID_W,
                                   final_norm=final_norm, casts=casts)
            if ctx_needed:
                xc, = _shortconv(xc, mixer(cx), tile=ctx_tile, shift=1)
        if ctx_needed:
            xc, = _conv_ffn(xc, ffn(cx), tile=ctx_tile, period=ctx_len, final_norm=False)
        if nxt_w:
            rg_w, ffn_w, nxt_w = nxt_w[:len(rg_src)], nxt_w[len(rg_src):], []
    return xl[None]
```
